```python
import jax
import jax.numpy as jnp
from jax import lax
import numpy as np

D_MODEL = 2048
BATCH = 16
SEQ = 2048
DEPTH = 4
DEC_BATCH = 2
DEC_SEQ = 8192
PAST_LEN = 128

N_MIXERS = 3
N_LAYERS_A = (DEPTH + 2) // 3
N_LAYERS_B = (DEPTH + 1) // 3
N_LAYERS_C = DEPTH // 3
NORM_EPS = 1e-6

D_RNN = D_MODEL
N_BLOCKS_A = 16
BLOCK_A = D_RNN // N_BLOCKS_A
CONV_W = 4
LRU_C = 8.0

N_HEADS_B = 16
KEY_B = 128
VAL_B = D_MODEL // N_HEADS_B
F_B = N_HEADS_B * KEY_B
CHUNK_B = 32

HEAD_C = 64
N_HEADS_C = D_MODEL // HEAD_C
DECAY_LORA = 96
ICL_LORA = 96
GATE_LORA = 256
LNX_EPS = 64e-5

N_EXPERTS = 16
D_EXPERT = 2048
CAPACITY_FACTOR = 2

kernel_name = 'hybrid_bidir_rglru_hgrn2_rwkv7_ecmoe'


def _rmsnorm(x, g):
    x32 = x.astype(jnp.float32)
    y = x32 * lax.rsqrt(jnp.mean(x32 * x32, axis=-1, keepdims=True) + NORM_EPS)
    return (y * g.astype(jnp.float32)).astype(x.dtype)


def _centred_dwconv(x, w, b):
    S = x.shape[1]
    left = CONV_W // 2
    xp = jnp.pad(x, ((0, 0), (left, CONV_W - 1 - left), (0, 0)))
    out = b
    for j in range(CONV_W):
        out = out + xp[:, j:j + S] * w[j]
    return out


def _linear_scan(a, u, reverse):
    def combine(left, right):
        a_l, u_l = left
        a_r, u_r = right
        return a_l * a_r, a_r * u_l + u_r
    _, hs = lax.associative_scan(combine, (a, u), reverse=reverse, axis=1)
    return hs


def _rglru_direction(xc, w_r, b_r, w_i, b_i, lam, reverse):
    B, S, _ = xc.shape
    xb = xc.reshape(B, S, N_BLOCKS_A, BLOCK_A)
    r = jax.nn.sigmoid(jnp.einsum('bsnc,ncd->bsnd', xb, w_r).reshape(B, S, D_RNN) + b_r)
    i = jax.nn.sigmoid(jnp.einsum('bsnc,ncd->bsnd', xb, w_i).reshape(B, S, D_RNN) + b_i)
    log_a = -LRU_C * r * jax.nn.softplus(-lam)
    u = jnp.sqrt(-jnp.expm1(2.0 * log_a)) * (i * xc)
    return _linear_scan(jnp.exp(log_a), u, reverse)


def mixer_rglru(h, w_in, conv_w, conv_b, gate_w, gate_b, lam, w_out):
    f32 = jnp.float32
    u = jnp.einsum('bsd,de->bse', h, w_in)
    gate, xr = u[..., :D_RNN], u[..., D_RNN:]
    xc = _centred_dwconv(xr.astype(f32), conv_w.astype(f32), conv_b.astype(f32))
    gw = gate_w.astype(f32)
    gb = gate_b.astype(f32)
    lm = lam.astype(f32)
    y = (_rglru_direction(xc, gw[0, 0], gb[0, 0], gw[0, 1], gb[0, 1], lm[0], False)
         + _rglru_direction(xc, gw[1, 0], gb[1, 0], gw[1, 1], gb[1, 1], lm[1], True))
    y = y * jax.nn.gelu(gate.astype(f32))
    return jnp.einsum('bse,ed->bsd', y.astype(h.dtype), w_out)


def _chunk_gated_recurrence(q, k, v, logf):
    B, S, H, K = q.shape
    V = v.shape[-1]
    n_chunks = S // CHUNK_B

    def to_chunks(t):
        return t.reshape(B, n_chunks, CHUNK_B, H, t.shape[-1]).transpose(1, 0, 3, 2, 4)

    lower = jnp.tril(jnp.ones((CHUNK_B, CHUNK_B), dtype=bool))[:, :, None]

    def step(state, inp):
        qc, kc, vc, gc = inp
        b = jnp.cumsum(gc, axis=2)
        inter = jnp.einsum('bhtk,bhkv->bhtv', qc * jnp.exp(b), state)
        diff = b[:, :, :, None, :] - b[:, :, None, :, :]
        dec = jnp.exp(jnp.where(lower, diff, -jnp.inf))
        scores = jnp.einsum('bhtk,bhtsk,bhsk->bhts', qc, dec, kc)
        out = inter + jnp.einsum('bhts,bhsv->bhtv', scores, vc)
        b_last = b[:, :, -1:, :]
        state = (jnp.exp(b_last[:, :, 0, :])[..., None] * state
                 + jnp.einsum('bhsk,bhsv->bhkv', kc * jnp.exp(b_last - b), vc))
        return state, out

    state0 = jnp.zeros((B, H, K, V), jnp.float32)
    _, out = lax.scan(step, state0, (to_chunks(q), to_chunks(k), to_chunks(v), to_chunks(logf)))
    return out.transpose(1, 0, 3, 2, 4).reshape(B, S, H, V)


def mixer_hgrn2(h, w_in, lb, gn_g, w_out):
    B, S, _ = h.shape
    f32 = jnp.float32
    u = jnp.einsum('bsd,de->bse', h, w_in).astype(f32)
    heads_k = lambda t: t.reshape(B, S, N_HEADS_B, KEY_B)
    q = heads_k(u[..., :F_B])
    z_fwd = heads_k(u[..., F_B:2 * F_B])
    z_bwd = heads_k(u[..., 2 * F_B:3 * F_B])
    iv = u[..., 3 * F_B:3 * F_B + D_MODEL].reshape(B, S, N_HEADS_B, VAL_B)
    g = u[..., 3 * F_B + D_MODEL:]
    lbh = lb.astype(f32).reshape(N_HEADS_B, KEY_B)
    f_fwd = lbh + (1.0 - lbh) * jax.nn.sigmoid(z_fwd)
    f_bwd = lbh + (1.0 - lbh) * jax.nn.sigmoid(z_bwd)
    flip = lambda t: jnp.flip(t, axis=1)
    o = (_chunk_gated_recurrence(q, 1.0 - f_fwd, iv, jnp.log(f_fwd))
         + flip(_chunk_gated_recurrence(flip(q), flip(1.0 - f_bwd), flip(iv), flip(jnp.log(f_bwd)))))
    o = _rmsnorm(o, gn_g)
    y = (o.reshape(B, S, D_MODEL) * jax.nn.silu(g)).astype(h.dtype)
    return jnp.einsum('bse,ed->bsd', y, w_out)


def _rwkv7_scan(r, w, k, v, kk, a, reverse):
    B, S, H, N = r.shape

    def step(state, inp):
        r_t, w_t, k_t, v_t, kk_t, a_t = inp
        sa = jnp.einsum('bhvk,bhk->bhv', state, -kk_t)
        state = (state * w_t[:, :, None, :] + sa[..., None] * (kk_t * a_t)[:, :, None, :]
                 + v_t[..., None] * k_t[:, :, None, :])
        return state, jnp.einsum('bhvk,bhk->bhv', state, r_t)

    seq_major = lambda t: jnp.swapaxes(t, 0, 1)
    state0 = jnp.zeros((B, H, N, N), jnp.float32)
    xs = (seq_major(r), seq_major(w), seq_major(k), seq_major(v), seq_major(kk), seq_major(a))
    _, y = lax.scan(step, state0, xs, reverse=reverse)
    return seq_major(y)


def mixer_rwkv7(h, mu, w_rkv, w0, w1, w2, a0, a1, a2, g1, g2, k_k, k_a, r_k, lnx_g, lnx_b, w_o):
    B, S, D = h.shape
    f32 = jnp.float32
    zero = jnp.zeros_like(h[:, :1])
    prev = jnp.concatenate([zero, h[:, :-1]], axis=1)
    nxt = jnp.concatenate([h[:, 1:], zero], axis=1)
    xx = 0.5 * (prev + nxt) - h
    mix = lambda n: h + xx * mu[n]
    xw, xa = mix(1), mix(4)
    r = jnp.einsum('bsd,de->bse', mix(0), w_rkv[0]).astype(f32)
    k = jnp.einsum('bsd,de->bse', mix(2), w_rkv[1]).astype(f32)
    v = jnp.einsum('bsd,de->bse', mix(3), w_rkv[2]).astype(f32)
    g = jnp.einsum('bsr,rd->bsd', jax.nn.sigmoid(jnp.einsum('bsd,dr->bsr', mix(5), g1)), g2).astype(f32)
    heads = lambda t: t.reshape(B, S, N_HEADS_C, HEAD_C)
    kk = heads(k * k_k.astype(f32))
    kk = kk / jnp.maximum(jnp.sqrt(jnp.sum(kk * kk, axis=-1, keepdims=True)), 1e-12)
    rh, vh = heads(r), heads(v)
    ys = []
    bonuses = []
    for d in range(2):
        lw = jnp.einsum('bsr,rd->bsd', jnp.tanh(jnp.einsum('bsd,dr->bsr', xw, w1[d])), w2[d]).astype(f32)
        w = -jax.nn.softplus(-(w0[d].astype(f32) + lw)) - 0.5
        decay = jnp.exp(-jnp.exp(w))
        la = jnp.einsum('bsr,rd->bsd', jnp.einsum('bsd,dr->bsr', xa, a1[d]), a2[d]).astype(f32)
        a = jax.nn.sigmoid(a0[d].astype(f32) + la)
        kd = heads(k * (1.0 + (a - 1.0) * k_a.astype(f32)))
        ys.append(_rwkv7_scan(rh, heads(decay), kd, vh, kk, heads(a), d == 1))
        bonuses.append(jnp.sum(rh * kd * r_k.astype(f32), axis=-1, keepdims=True) * vh)
    y = ys[0] + ys[1]
    mean = jnp.mean(y, axis=-1, keepdims=True)
    var = jnp.mean(jnp.square(y - mean), axis=-1, keepdims=True)
    yn = ((y - mean) * lax.rsqrt(var + LNX_EPS)).reshape(B, S, D) * lnx_g.astype(f32) + lnx_b.astype(f32)
    out = (yn + (bonuses[0] + bonuses[1]).reshape(B, S, D)) * g
    return jnp.einsum('bsd,de->bse', out.astype(h.dtype), w_o)


def _ec_moe(h, router_w, w_gate, w_up, w_down):
    B, S, D = h.shape
    T = B * S
    cap = CAPACITY_FACTOR * T // N_EXPERTS
    ht = h.reshape(T, D)
    probs = jax.nn.softmax(jnp.einsum('td,de->te', ht, router_w).astype(jnp.float32), axis=-1)
    gates, idx = lax.top_k(probs.T, cap)
    xe = ht[idx]
    hid = jax.nn.silu(jnp.einsum('ecd,edf->ecf', xe, w_gate)) * jnp.einsum('ecd,edf->ecf', xe, w_up)
    ye = jnp.einsum('ecf,efd->ecd', hid, w_down) * gates[..., None].astype(h.dtype)
    out = jnp.zeros((T, D), h.dtype).at[idx.reshape(-1)].add(ye.reshape(-1, D))
    return out.reshape(B, S, D)


def _trunk(x, c, P):
    lower = jnp.cumsum(jax.nn.softmax(P['hg_lb'].astype(jnp.float32), axis=0), axis=0)
    lbs = lower - lower[:1]
    sc = jax.nn.silu(c)
    for layer in range(DEPTH):
        mod = jnp.einsum('bd,de->be', sc, P['ada_w'][layer]) + P['ada_b'][layer]
        sh1, sc1, gt1, sh2, sc2, gt2 = jnp.split(mod, 6, axis=-1)
        h = _rmsnorm(x, P['norm_g'][layer, 0]) * (1.0 + sc1[:, None]) + sh1[:, None]
        kind = layer % N_MIXERS
        j = layer // N_MIXERS
        if kind == 0:
            m = mixer_rglru(h, P['lru_w_in'][j], P['lru_conv_w'][j], P['lru_conv_b'][j],
                            P['lru_gate_w'][j], P['lru_gate_b'][j], P['lru_lam'][j], P['lru_w_out'][j])
        elif kind == 1:
            m = mixer_hgrn2(h, P['hg_w_in'][j], lbs[layer], P['hg_gn_g'][j], P['hg_w_out'][j])
        else:
            m = mixer_rwkv7(h, P['r7_mu'][j], P['r7_w_rkv'][j], P['r7_w0'][j], P['r7_w1'][j], P['r7_w2'][j],
                            P['r7_a0'][j], P['r7_a1'][j], P['r7_a2'][j], P['r7_g1'][j], P['r7_g2'][j],
                            P['r7_k_k'][j], P['r7_k_a'][j], P['r7_r_k'][j], P['r7_lnx_g'][j],
                            P['r7_lnx_b'][j], P['r7_w_o'][j])
        x = x + gt1[:, None] * m
        h = _rmsnorm(x, P['norm_g'][layer, 1]) * (1.0 + sc2[:, None]) + sh2[:, None]
        x = x + gt2[:, None] * _ec_moe(h, P['moe_router'][layer], P['moe_w_gate'][layer],
                                        P['moe_w_up'][layer], P['moe_w_down'][layer])
    return _rmsnorm(x, P['final_g'])


def setup_inputs(seed: int = 0) -> dict:
    key = jax.random.key(seed)
    keys = jax.random.split(key, 64)
    counter = [0]
    f32 = jnp.float32

    def nk():
        counter[0] += 1
        return keys[counter[0] - 1]

    def nrm(shape, scale):
        return jax.random.normal(nk(), shape, f32) * scale

    def uni(shape, lo, hi):
        return jax.random.uniform(nk(), shape, f32, lo, hi)

    D = D_MODEL
    inv = D ** -0.5
    s_lru = uni((N_LAYERS_A, 2, D_RNN), 0.9, 0.999) ** (1.0 / LRU_C)
    return {
        'x_prompt': nrm((BATCH, SEQ, D), 1.0),
        'x_sample': nrm((DEC_BATCH, DEC_SEQ, D), 1.0),
        'c_prompt': nrm((BATCH, D), 1.0),
        'c_sample': nrm((DEC_BATCH, D), 1.0),
        'ada_w': nrm((DEPTH, D, 6 * D), 0.5 * inv),
        'ada_b': nrm((DEPTH, 6 * D), 0.02),
        'norm_g': 1.0 + nrm((DEPTH, 2, D), 0.02),
        'final_g': 1.0 + nrm((D,), 0.02),
        'lru_w_in': nrm((N_LAYERS_A, D, 2 * D_RNN), inv),
        'lru_conv_w': nrm((N_LAYERS_A, CONV_W, D_RNN), CONV_W ** -0.5),
        'lru_conv_b': nrm((N_LAYERS_A, D_RNN), 0.02),
        'lru_gate_w': nrm((N_LAYERS_A, 2, 2, N_BLOCKS_A, BLOCK_A, BLOCK_A), BLOCK_A ** -0.5),
        'lru_gate_b': nrm((N_LAYERS_A, 2, 2, D_RNN), 0.02),
        'lru_lam': jnp.log(s_lru) - jnp.log1p(-s_lru),
        'lru_w_out': nrm((N_LAYERS_A, D_RNN, D), D_RNN ** -0.5),
        'hg_w_in': nrm((N_LAYERS_B, D, 3 * F_B + 2 * D), inv),
        'hg_lb': nrm((DEPTH, F_B), 1.0),
        'hg_gn_g': 1.0 + nrm((N_LAYERS_B, VAL_B), 0.02),
        'hg_w_out': nrm((N_LAYERS_B, D, D), inv),
        'r7_mu': uni((N_LAYERS_C, 6, D), 0.0, 1.0),
        'r7_w_rkv': nrm((N_LAYERS_C, 3, D, D), inv),
        'r7_w0': uni((N_LAYERS_C, 2, D), -6.0, -1.0),
        'r7_w1': nrm((N_LAYERS_C, 2, D, DECAY_LORA), inv),
        'r7_w2': nrm((N_LAYERS_C, 2, DECAY_LORA, D), 0.1 * DECAY_LORA ** -0.5),
        'r7_a0': nrm((N_LAYERS_C, 2, D), 0.1),
        'r7_a1': nrm((N_LAYERS_C, 2, D, ICL_LORA), inv),
        'r7_a2': nrm((N_LAYERS_C, 2, ICL_LORA, D), 0.1 * ICL_LORA ** -0.5),
        'r7_g1': nrm((N_LAYERS_C, D, GATE_LORA), inv),
        'r7_g2': nrm((N_LAYERS_C, GATE_LORA, D), GATE_LORA ** -0.5),
        'r7_k_k': 0.85 + nrm((N_LAYERS_C, D), 0.02),
        'r7_k_a': 1.0 + nrm((N_LAYERS_C, D), 0.02),
        'r7_r_k': nrm((N_LAYERS_C, N_HEADS_C, HEAD_C), 0.1),
        'r7_lnx_g': 1.0 + nrm((N_LAYERS_C, D), 0.02),
        'r7_lnx_b': nrm((N_LAYERS_C, D), 0.02),
        'r7_w_o': nrm((N_LAYERS_C, D, D), inv),
        'moe_router': nrm((DEPTH, D, N_EXPERTS), inv),
        'moe_w_gate': nrm((DEPTH, N_EXPERTS, D, D_EXPERT), inv),
        'moe_w_up': nrm((DEPTH, N_EXPERTS, D, D_EXPERT), inv),
        'moe_w_down': nrm((DEPTH, N_EXPERTS, D_EXPERT, D), D_EXPERT ** -0.5),
    }


def reference(x_prompt, x_sample, c_prompt, c_sample, ada_w, ada_b, norm_g, final_g,
              lru_w_in, lru_conv_w, lru_conv_b, lru_gate_w, lru_gate_b, lru_lam, lru_w_out,
              hg_w_in, hg_lb, hg_gn_g, hg_w_out,
              r7_mu, r7_w_rkv, r7_w0, r7_w1, r7_w2, r7_a0, r7_a1, r7_a2, r7_g1, r7_g2,
              r7_k_k, r7_k_a, r7_r_k, r7_lnx_g, r7_lnx_b, r7_w_o,
              moe_router, moe_w_gate, moe_w_up, moe_w_down):
    P = dict(ada_w=ada_w, ada_b=ada_b, norm_g=norm_g, final_g=final_g,
             lru_w_in=lru_w_in, lru_conv_w=lru_conv_w, lru_conv_b=lru_conv_b,
             lru_gate_w=lru_gate_w, lru_gate_b=lru_gate_b, lru_lam=lru_lam, lru_w_out=lru_w_out,
             hg_w_in=hg_w_in, hg_lb=hg_lb, hg_gn_g=hg_gn_g, hg_w_out=hg_w_out,
             r7_mu=r7_mu, r7_w_rkv=r7_w_rkv, r7_w0=r7_w0, r7_w1=r7_w1, r7_w2=r7_w2,
             r7_a0=r7_a0, r7_a1=r7_a1, r7_a2=r7_a2, r7_g1=r7_g1, r7_g2=r7_g2,
             r7_k_k=r7_k_k, r7_k_a=r7_k_a, r7_r_k=r7_r_k, r7_lnx_g=r7_lnx_g,
             r7_lnx_b=r7_lnx_b, r7_w_o=r7_w_o,
             moe_router=moe_router, moe_w_gate=moe_w_gate, moe_w_up=moe_w_up, moe_w_down=moe_w_down)
    y_prompt = _trunk(x_prompt, c_prompt, P)
    y_sample = _trunk(x_sample, c_sample, P)
    return (y_prompt, y_sample)
```

```python
import functools

import jax
import jax.numpy as jnp
from jax import lax
from jax.experimental import pallas as pl
from jax.experimental.pallas import tpu as pltpu

F32 = jnp.float32
BF16 = jnp.bfloat16

NORM_EPS = 1e-6
LNX_EPS = 64e-5
LRU_C = 8.0
CAPACITY_FACTOR = 2
HG_CHUNK = 32
EXP_CLAMP = 80.0

LANE = 128
SUBLANE = 8
VMEM_LIMIT = 48 * 1024 * 1024


def _cparams(sem):
    return pltpu.CompilerParams(dimension_semantics=sem, vmem_limit_bytes=VMEM_LIMIT)


def _pick(n, cands):
    for c in cands:
        if n % c == 0:
            return c
    return n


def _sigmoid(x):
    return 1.0 / (1.0 + jnp.exp(-x))


def _silu(x):
    return x * _sigmoid(x)


def _gelu_tanh(x):
    return 0.5 * x * (1.0 + jnp.tanh(0.7978845608028654 * (x + 0.044715 * (x * x * x))))


def _neg_expm1(x):
    p = 1.0 + x * (1.0 / 8.0)
    for n in (7.0, 6.0, 5.0, 4.0, 3.0, 2.0):
        p = 1.0 + (x * (1.0 / n)) * p
    return jnp.where(x > -0.35, -(x * p), 1.0 - jnp.exp(x))


def _dot_exact_lhs(a_bf16, x):
    x1 = x.astype(BF16)
    r1 = x - x1.astype(F32)
    x2 = r1.astype(BF16)
    x3 = (r1 - x2.astype(F32)).astype(BF16)
    d = functools.partial(jnp.dot, preferred_element_type=F32)
    return d(a_bf16, x1) + d(a_bf16, x2) + d(a_bf16, x3)


def _ada_body(c_ref, w_ref, b_ref, o_ref):
    x = _silu(c_ref[...])
    acc = jnp.dot(x, w_ref[0], preferred_element_type=F32, precision=lax.Precision.HIGHEST)
    o_ref[0] = acc + b_ref[0]


def _ada_mod(c_all, ada_w, ada_b):
    L, D, N = ada_w.shape
    Mp = c_all.shape[0]
    tn = _pick(N, (1024, 512, 256, 128))
    return pl.pallas_call(
        _ada_body,
        grid=(L, N // tn),
        in_specs=[
            pl.BlockSpec((Mp, D), lambda l, j: (0, 0)),
            pl.BlockSpec((1, D, tn), lambda l, j: (l, 0, j)),
            pl.BlockSpec((1, 1, tn), lambda l, j: (l, 0, j)),
        ],
        out_specs=pl.BlockSpec((1, Mp, tn), lambda l, j: (l, 0, j)),
        out_shape=jax.ShapeDtypeStruct((L, Mp, N), F32),
        compiler_params=_cparams(("parallel", "parallel")),
        name="ada_mod",
    )(c_all, ada_w, ada_b.reshape(L, 1, N))


def _norm_body(x_ref, g_ref, sc_ref, sh_ref, *rest, with_router):
    x = x_ref[...]
    ms = jnp.mean(x * x, axis=-1, keepdims=True)
    y = x * lax.rsqrt(ms + NORM_EPS) * g_ref[...]
    h = y * (1.0 + sc_ref[0]) + sh_ref[0]
    if with_router:
        rwt_ref, o_ref, p_ref = rest
        logits = lax.dot_general(rwt_ref[...], h, (((1,), (1,)), ((), ())),
                                 preferred_element_type=F32, precision=lax.Precision.HIGHEST)
        m = jnp.max(logits, axis=0, keepdims=True)
        e = jnp.exp(logits - m)
        p_ref[...] = e / jnp.sum(e, axis=0, keepdims=True)
    else:
        (o_ref,) = rest
    o_ref[...] = h.astype(o_ref.dtype)


def _norm_mod(x2, g, sc, sh, S, out_dtype, router_wt=None):
    T, D = x2.shape
    B = sc.shape[0]
    tm = _pick(S, (512, 256, 128, 64, 32, 16, 8))
    bmap = lambda i: ((i * tm) // S, 0, 0)
    in_specs = [
        pl.BlockSpec((tm, D), lambda i: (i, 0)),
        pl.BlockSpec((1, D), lambda i: (0, 0)),
        pl.BlockSpec((1, 1, D), bmap),
        pl.BlockSpec((1, 1, D), bmap),
    ]
    args = [x2, g.reshape(1, D), sc.reshape(B, 1, D), sh.reshape(B, 1, D)]
    out_specs = pl.BlockSpec((tm, D), lambda i: (i, 0))
    out_shape = jax.ShapeDtypeStruct((T, D), out_dtype)
    if router_wt is not None:
        E = router_wt.shape[0]
        in_specs.append(pl.BlockSpec((E, D), lambda i: (0, 0)))
        args.append(router_wt)
        out_specs = (out_specs, pl.BlockSpec((E, tm), lambda i: (0, i)))
        out_shape = (out_shape, jax.ShapeDtypeStruct((E, T), F32))
    return pl.pallas_call(
        functools.partial(_norm_body, with_router=router_wt is not None),
        grid=(T // tm,),
        in_specs=in_specs,
        out_specs=out_specs,
        out_shape=out_shape,
        compiler_params=_cparams(("parallel",)),
        name="norm_mod",
    )(*args)


def _mm_body(*refs, act, has_mix, has_res):
    it = iter(refs)
    x_ref = next(it)
    x = x_ref[...]
    if has_mix:
        xx_ref = next(it)
        mu_ref = next(it)
        x = x.astype(F32) + xx_ref[...].astype(F32) * mu_ref[...]
    w_ref = next(it)
    if act == "tanh":
        x = jnp.tanh(x.astype(F32))
    elif act == "sigmoid":
        x = _sigmoid(x.astype(F32))
    acc = jnp.dot(x.astype(BF16), w_ref[...], preferred_element_type=F32)
    if has_res:
        res_ref = next(it)
        gt_ref = next(it)
        acc = res_ref[...] + gt_ref[0] * acc
    o_ref = next(it)
    o_ref[...] = acc.astype(o_ref.dtype)


def _matmul(x, w, *, out_dtype=F32, act=None, mix=None, res=None, gate=None, S=None,
            x_col_block=0, tm=None, tn=None):
    M = x.shape[0]
    K, N = w.shape
    tm = tm or _pick(M if S is None else S, (512, 256, 128, 64, 32, 16, 8))
    tn = tn or _pick(N, (1024, 512, 256, 128))
    cb = x_col_block
    in_specs = [pl.BlockSpec((tm, K), lambda j, i: (i, cb))]
    args = [x]
    if mix is not None:
        xx, mu = mix
        in_specs += [pl.BlockSpec((tm, K), lambda j, i: (i, 0)),
                     pl.BlockSpec((1, K), lambda j, i: (0, 0))]
        args += [xx, mu.reshape(1, K)]
    in_specs.append(pl.BlockSpec((K, tn), lambda j, i: (0, j)))
    args.append(w)
    if res is not None:
        B = gate.shape[0]
        in_specs += [pl.BlockSpec((tm, tn), lambda j, i: (i, j)),
                     pl.BlockSpec((1, 1, tn), lambda j, i: ((i * tm) // S, 0, j))]
        args += [res, gate.reshape(B, 1, N)]
    return pl.pallas_call(
        functools.partial(_mm_body, act=act, has_mix=mix is not None, has_res=res is not None),
        grid=(N // tn, M // tm),
        in_specs=in_specs,
        out_specs=pl.BlockSpec((tm, tn), lambda j, i: (i, j)),
        out_shape=jax.ShapeDtypeStruct((M, N), out_dtype),
        compiler_params=_cparams(("parallel", "parallel")),
        name="matmul",
    )(*args)


def _tile_scan(a, u, reverse):
    R = a.shape[0]
    r8 = lax.broadcasted_iota(jnp.int32, a.shape, 0) & (SUBLANE - 1)
    for d in (1, 2, 4):
        if reverse:
            a_sh = pltpu.roll(a, R - d, axis=0)
            u_sh = pltpu.roll(u, R - d, axis=0)
            m = r8 < SUBLANE - d
        else:
            a_sh = pltpu.roll(a, d, axis=0)
            u_sh = pltpu.roll(u, d, axis=0)
            m = r8 >= d
        u = jnp.where(m, a * u_sh + u, u)
        a = jnp.where(m, a * a_sh, a)
    return a, u


def _rglru_body(*refs, reverse, Tc, nb, bs):
    if reverse:
        (xr_ref, pv_ref, nx_ref, cw_ref, cb_ref, gw_ref, gb_ref, sp_ref,
         hf_ref, gate_ref, o_ref, carry_ref) = refs
    else:
        (xr_ref, pv_ref, nx_ref, cw_ref, cb_ref, gw_ref, gb_ref, sp_ref,
         o_ref, carry_ref) = refs
    c = pl.program_id(2)
    nc = pl.num_programs(2)
    tpos = nc - 1 - c if reverse else c

    @pl.when(c == 0)
    def _():
        carry_ref[...] = jnp.zeros_like(carry_ref)

    x = xr_ref[0]
    pv = jnp.where(tpos > 0, pv_ref[0], 0.0)
    nx = jnp.where(tpos < nc - 1, nx_ref[0], 0.0)
    xp = jnp.concatenate([pv, x, nx], axis=0)
    cw = cw_ref[...]
    xc = cb_ref[...]
    for j in range(4):
        xc = xc + xp[SUBLANE - 2 + j:SUBLANE - 2 + j + Tc] * cw[j:j + 1]
    xcb = xc.astype(BF16)
    rs, gs = [], []
    for j in range(nb):
        blk = xcb[:, j * bs:(j + 1) * bs]
        rs.append(jnp.dot(blk, gw_ref[0, j], preferred_element_type=F32))
        gs.append(jnp.dot(blk, gw_ref[1, j], preferred_element_type=F32))
    r = _sigmoid(jnp.concatenate(rs, axis=1) + gb_ref[0:1])
    ig = _sigmoid(jnp.concatenate(gs, axis=1) + gb_ref[1:2])
    log_a = (-LRU_C) * r * sp_ref[...]
    a = jnp.exp(log_a)
    u = jnp.sqrt(_neg_expm1(2.0 * log_a)) * (ig * xc)
    a_loc, u_loc = _tile_scan(a, u, reverse)

    h_prev = carry_ref[0:1]
    n_tiles = Tc // SUBLANE
    order = range(n_tiles - 1, -1, -1) if reverse else range(n_tiles)
    for k in order:
        sl = slice(k * SUBLANE, (k + 1) * SUBLANE)
        ht = u_loc[sl] + a_loc[sl] * h_prev
        h_prev = ht[0:1] if reverse else ht[SUBLANE - 1:SUBLANE]
        if reverse:
            y = (hf_ref[0, sl, :] + ht) * _gelu_tanh(gate_ref[0, sl, :])
            o_ref[0, sl, :] = y.astype(o_ref.dtype)
        else:
            o_ref[0, sl, :] = ht
    carry_ref[0:1] = h_prev


def _rglru_dir(u3, conv_w, conv_b, gw, gb, sp, reverse, hf=None):
    B, S, two_dr = u3.shape
    Dr = two_dr // 2
    bs = gw.shape[-1]
    C = _pick(Dr, (512, 256, 128))
    Tc = _pick(S, (256, 128, 64, 32, 16, 8))
    nb = C // bs
    ncb = Dr // C
    nc = S // Tc
    t8 = Tc // SUBLANE
    n8 = S // SUBLANE

    def tmap(c):
        return nc - 1 - c if reverse else c

    in_specs = [
        pl.BlockSpec((1, Tc, C), lambda b, n, c: (b, tmap(c), ncb + n)),
        pl.BlockSpec((1, SUBLANE, C), lambda b, n, c: (b, jnp.maximum(tmap(c) * t8 - 1, 0), ncb + n)),
        pl.BlockSpec((1, SUBLANE, C), lambda b, n, c: (b, jnp.minimum((tmap(c) + 1) * t8, n8 - 1), ncb + n)),
        pl.BlockSpec((conv_w.shape[0], C), lambda b, n, c: (0, n)),
        pl.BlockSpec((1, C), lambda b, n, c: (0, n)),
        pl.BlockSpec((2, nb, bs, bs), lambda b, n, c: (0, n, 0, 0)),
        pl.BlockSpec((2, C), lambda b, n, c: (0, n)),
        pl.BlockSpec((1, C), lambda b, n, c: (0, n)),
    ]
    args = [u3, u3, u3, conv_w, conv_b.reshape(1, Dr), gw, gb, sp]
    if reverse:
        in_specs += [pl.BlockSpec((1, Tc, C), lambda b, n, c: (b, tmap(c), n)),
                     pl.BlockSpec((1, Tc, C), lambda b, n, c: (b, tmap(c), n))]
        args += [hf, u3]
        out_dtype = BF16
    else:
        out_dtype = F32
    return pl.pallas_call(
        functools.partial(_rglru_body, reverse=reverse, Tc=Tc, nb=nb, bs=bs),
        grid=(B, ncb, nc),
        in_specs=in_specs,
        out_specs=pl.BlockSpec((1, Tc, C), lambda b, n, c: (b, tmap(c), n)),
        out_shape=jax.ShapeDtypeStruct((B, S, Dr), out_dtype),
        scratch_shapes=[pltpu.VMEM((SUBLANE, C), F32)],
        compiler_params=_cparams(("parallel", "parallel", "arbitrary")),
        name="rglru_bwd" if reverse else "rglru_fwd",
    )(*args)


def _mixer_rglru(h2, B, S, p, x2, gt):
    D = h2.shape[1]
    u = _matmul(h2, p["w_in"], S=S)
    Dr = u.shape[1] // 2
    u3 = u.reshape(B, S, 2 * Dr)
    hf = _rglru_dir(u3, p["conv_w"], p["conv_b"], p["gw"][0], p["gb"][0], p["sp"][0:1], False)
    y = _rglru_dir(u3, p["conv_w"], p["conv_b"], p["gw"][1], p["gb"][1], p["sp"][1:2], True, hf=hf)
    return _matmul(y.reshape(B * S, Dr), p["w_out"], res=x2, gate=gt, S=S)


def _hgrn2_body(*refs, reverse, Tc, C):
    if reverse:
        q_ref, z_ref, v_ref, lb_ref, of_ref, g_ref, gn_ref, o_ref, st_ref = refs
    else:
        q_ref, z_ref, v_ref, lb_ref, o_ref, st_ref = refs
    c = pl.program_id(2)

    @pl.when(c == 0)
    def _():
        st_ref[...] = jnp.zeros_like(st_ref)

    lb = lb_ref[...]
    row = lax.broadcasted_iota(jnp.int32, (C, C), 0)
    col = lax.broadcasted_iota(jnp.int32, (C, C), 1)
    keep = (col >= row) if reverse else (col <= row)
    tri = jnp.where(keep, 1.0, 0.0).astype(BF16)
    mid = C // 2
    nsub = Tc // C
    order = range(nsub - 1, -1, -1) if reverse else range(nsub)
    st = st_ref[...]
    for j in order:
        sl = slice(j * C, (j + 1) * C)
        q = q_ref[0, sl, :]
        f = lb + (1.0 - lb) * _sigmoid(z_ref[0, sl, :])
        k = 1.0 - f
        v = v_ref[0, sl, :]
        b = _dot_exact_lhs(tri, jnp.log(f))
        bm = b[mid:mid + 1]
        qt = (q * jnp.exp(jnp.minimum(b - bm, EXP_CLAMP))).astype(BF16)
        kt = (k * jnp.exp(jnp.minimum(bm - b, EXP_CLAMP))).astype(BF16)
        sc = lax.dot_general(qt, kt, (((1,), (1,)), ((), ())), preferred_element_type=F32)
        sc = jnp.where(keep, sc, 0.0).astype(BF16)
        vb = v.astype(BF16)
        qe = (q * jnp.exp(b)).astype(BF16)
        o = (lax.dot_general(qe, st.astype(BF16), (((1,), (1,)), ((), ())), preferred_element_type=F32)
             + jnp.dot(sc, vb, preferred_element_type=F32))
        b_end = b[0:1] if reverse else b[C - 1:C]
        kh = (k * jnp.exp(b_end - b)).astype(BF16)
        st = (st * jnp.exp(b_end)
              + lax.dot_general(vb, kh, (((0,), (0,)), ((), ())), preferred_element_type=F32))
        if reverse:
            ot = of_ref[0, sl, :] + o
            ms = jnp.mean(ot * ot, axis=-1, keepdims=True)
            on = ot * lax.rsqrt(ms + NORM_EPS) * gn_ref[...]
            o_ref[0, sl, :] = (on * _silu(g_ref[0, sl, :])).astype(o_ref.dtype)
        else:
            o_ref[0, sl, :] = o
    st_ref[...] = st


def _hgrn2_dir(u3, lb, gn_g, H, K, V, reverse, of=None):
    B, S, _ = u3.shape
    Tc = _pick(S, (256, 128, 64, 32))
    C = min(HG_CHUNK, Tc)
    nc = S // Tc
    zoff = (2 if reverse else 1) * H
    voff = (3 * H * K) // V

    def tmap(c):
        return nc - 1 - c if reverse else c

    in_specs = [
        pl.BlockSpec((1, Tc, K), lambda b, h, c: (b, tmap(c), h)),
        pl.BlockSpec((1, Tc, K), lambda b, h, c: (b, tmap(c), zoff + h)),
        pl.BlockSpec((1, Tc, V), lambda b, h, c: (b, tmap(c), voff + h)),
        pl.BlockSpec((1, K), lambda b, h, c: (0, h)),
    ]
    args = [u3, u3, u3, lb]
    if reverse:
        in_specs += [pl.BlockSpec((1, Tc, V), lambda b, h, c: (b, tmap(c), h)),
                     pl.BlockSpec((1, Tc, V), lambda b, h, c: (b, tmap(c), voff + H + h)),
                     pl.BlockSpec((1, V), lambda b, h, c: (0, 0))]
        args += [of, u3, gn_g]
        out_dtype = BF16
    else:
        out_dtype = F32
    return pl.pallas_call(
        functools.partial(_hgrn2_body, reverse=reverse, Tc=Tc, C=C),
        grid=(B, H, nc),
        in_specs=in_specs,
        out_specs=pl.BlockSpec((1, Tc, V), lambda b, h, c: (b, tmap(c), h)),
        out_shape=jax.ShapeDtypeStruct((B, S, H * V), out_dtype),
        scratch_shapes=[pltpu.VMEM((V, K), F32)],
        compiler_params=_cparams(("parallel", "parallel", "arbitrary")),
        name="hgrn2_bwd" if reverse else "hgrn2_fwd",
    )(*args)


def _mixer_hgrn2(h2, B, S, p, lb, x2, gt):
    D = h2.shape[1]
    V = p["gn_g"].shape[-1]
    H = D // V
    K = lb.shape[-1] // H
    u = _matmul(h2, p["w_in"], S=S)
    u3 = u.reshape(B, S, u.shape[1])
    of = _hgrn2_dir(u3, lb, p["gn_g"], H, K, V, False)
    y = _hgrn2_dir(u3, lb, p["gn_g"], H, K, V, True, of=of)
    return _matmul(y.reshape(B * S, D), p["w_out"], res=x2, gate=gt, S=S)


def _rwkv_scan_body(w_ref, nkk_ref, kka_ref, kd_ref, r_ref, v_ref, y_ref, st_ref, *, Tc, N):
    c = pl.program_id(1)

    @pl.when(c == 0)
    def _():
        st_ref[...] = jnp.zeros_like(st_ref)

    nv = N // SUBLANE

    def step(t, carry):
        vv = v_ref[t]
        sas = [jnp.zeros_like(vv), jnp.zeros_like(vv)]
        for k in range(N):
            sas[k % 2] = sas[k % 2] + st_ref[k] * nkk_ref[t, k:k + 1, :]
        sa = sas[0] + sas[1]
        ys = [jnp.zeros_like(vv), jnp.zeros_like(vv)]
        for k in range(N):
            s_new = (st_ref[k] * w_ref[t, k:k + 1, :] + sa * kka_ref[t, k:k + 1, :]
                     + vv * kd_ref[t, k:k + 1, :])
            st_ref[k] = s_new
            ys[k % 2] = ys[k % 2] + s_new * r_ref[t, k:k + 1, :]
        y_ref[t] = ys[0] + ys[1]
        return carry

    lax.fori_loop(0, Tc, step, 0)


def _rwkv_scan(w, nkk, kka, kd, r, v):
    S, N, NC = w.shape
    Tc = _pick(S, (32, 16, 8))
    spec = pl.BlockSpec((Tc, N, LANE), lambda g, c: (c, 0, g))
    return pl.pallas_call(
        functools.partial(_rwkv_scan_body, Tc=Tc, N=N),
        grid=(NC // LANE, S // Tc),
        in_specs=[spec] * 6,
        out_specs=spec,
        out_shape=jax.ShapeDtypeStruct((S, N, NC), F32),
        scratch_shapes=[pltpu.VMEM((N, N, LANE), F32)],
        compiler_params=_cparams(("parallel", "arbitrary")),
        name="rwkv7_scan",
    )(w, nkk, kka, kd, r, v)


def _mixer_rwkv7(h2, B, S, p, x2, gt):
    T, D = h2.shape
    H, N = p["r_k"].shape
    h3 = h2.reshape(B, S, D)
    zero = jnp.zeros_like(h3[:, :1])
    prev = jnp.concatenate([zero, h3[:, :-1]], axis=1)
    nxt = jnp.concatenate([h3[:, 1:], zero], axis=1)
    xx = (0.5 * (prev + nxt) - h3).reshape(T, D)
    mu = p["mu"]
    mm = lambda n, w, **kw: _matmul(h2, w, mix=(xx, mu[n]), S=S, **kw)
    r = mm(0, p["w_r"])
    k = mm(2, p["w_k"])
    v = mm(3, p["w_v"])
    tw = mm(1, p["w1"])
    ta = mm(4, p["a1"])
    tg = mm(5, p["g1"])
    g = _matmul(tg, p["g2"], act="sigmoid", S=S)
    kk = (k * p["k_k"]).reshape(T, H, N)
    kk = kk / jnp.maximum(jnp.sqrt(jnp.sum(kk * kk, axis=-1, keepdims=True)), 1e-12)
    kk = kk.reshape(T, D)
    rh = r.reshape(T, H, N)
    vh = v.reshape(T, H, N)
    Rp = p["w2"].shape[1]
    per_dir = []
    bonus = 0.0
    for d in range(2):
        lw = _matmul(tw, p["w2"][d], act="tanh", S=S, x_col_block=d)
        wl = -jax.nn.softplus(-(p["w0"][d] + lw)) - 0.5
        decay = jnp.exp(-jnp.exp(wl))
        la = _matmul(ta, p["a2"][d], S=S, x_col_block=d)
        a = _sigmoid(p["a0"][d] + la)
        kd = k * (1.0 + (a - 1.0) * p["k_a"])
        bonus = bonus + jnp.sum(rh * kd.reshape(T, H, N) * p["r_k"], axis=-1, keepdims=True) * vh
        per_dir.append((decay, kk * a, kd))

    def chains(t2, flip):
        t4 = t2.reshape(B, S, H, N)
        if flip:
            t4 = jnp.flip(t4, axis=1)
        return t4.transpose(1, 3, 0, 2).reshape(S, N, B * H)

    def both(fwd, bwd):
        arr = jnp.concatenate([chains(fwd, False), chains(bwd, True)], axis=-1)
        pad = (-arr.shape[-1]) % LANE
        return jnp.pad(arr, ((0, 0), (0, 0), (0, pad))) if pad else arr

    ys = _rwkv_scan(both(per_dir[0][0], per_dir[1][0]), both(-kk, -kk),
                    both(per_dir[0][1], per_dir[1][1]), both(per_dir[0][2], per_dir[1][2]),
                    both(r, r), both(v, v))
    BH = B * H
    yf = ys[:, :, :BH].reshape(S, N, B, H).transpose(2, 0, 3, 1)
    yb = jnp.flip(ys[:, :, BH:2 * BH].reshape(S, N, B, H).transpose(2, 0, 3, 1), axis=1)
    y = (yf + yb).reshape(T, H, N)
    mean = jnp.mean(y, axis=-1, keepdims=True)
    var = jnp.mean(jnp.square(y - mean), axis=-1, keepdims=True)
    yn = ((y - mean) * lax.rsqrt(var + LNX_EPS)).reshape(T, D) * p["lnx_g"] + p["lnx_b"]
    out = ((yn + bonus.reshape(T, D)) * g).astype(BF16)
    return _matmul(out, p["w_o"], res=x2, gate=gt, S=S)


def _ffn_body(x_ref, wg_ref, wu_ref, wd_ref, gt_ref, o_ref):
    f = pl.program_id(2)
    x = x_ref[0]
    g = jnp.dot(x, wg_ref[0], preferred_element_type=F32)
    u = jnp.dot(x, wu_ref[0], preferred_element_type=F32)
    hid = (_silu(g) * u).astype(BF16)
    part = jnp.dot(hid, wd_ref[0], preferred_element_type=F32)

    @pl.when(f == 0)
    def _():
        o_ref[0] = part

    @pl.when(f > 0)
    def _():
        o_ref[0] = o_ref[0] + part

    @pl.when(f == pl.num_programs(2) - 1)
    def _():
        o_ref[0] = o_ref[0] * gt_ref[0]


def _moe_ffn(xe, wg, wu, wd, gates):
    E, cap, D = xe.shape
    Fd = wg.shape[2]
    tm = _pick(cap, (1024, 512, 256, 128, 64, 32, 16, 8))
    tf = _pick(Fd, (256, 128))
    return pl.pallas_call(
        _ffn_body,
        grid=(E, cap // tm, Fd // tf),
        in_specs=[
            pl.BlockSpec((1, tm, D), lambda e, m, f: (e, m, 0)),
            pl.BlockSpec((1, D, tf), lambda e, m, f: (e, 0, f)),
            pl.BlockSpec((1, D, tf), lambda e, m, f: (e, 0, f)),
            pl.BlockSpec((1, tf, D), lambda e, m, f: (e, f, 0)),
            pl.BlockSpec((1, tm, 1), lambda e, m, f: (e, m, 0)),
        ],
        out_specs=pl.BlockSpec((1, tm, D), lambda e, m, f: (e, m, 0)),
        out_shape=jax.ShapeDtypeStruct((E, cap, D), F32),
        compiler_params=_cparams(("parallel", "parallel", "arbitrary")),
        name="moe_ffn",
    )(xe, wg, wu, wd, gates.reshape(E, cap, 1))


def _ec_moe(h2, probs_t, p):
    T, D = h2.shape
    E = probs_t.shape[0]
    cap = CAPACITY_FACTOR * T // E
    gates, idx = lax.top_k(probs_t, cap)
    xe = h2[idx]
    ye = _moe_ffn(xe, p["w_gate"], p["w_up"], p["w_down"], gates)
    return jnp.zeros((T, D), F32).at[idx.reshape(-1)].add(ye.reshape(-1, D))


def _trunk(x, mod, P):
    B, S, D = x.shape
    T = B * S
    x2 = x.reshape(T, D)
    L = mod.shape[0]
    for layer in range(L):
        sh1, sc1, gt1, sh2, sc2, gt2 = jnp.split(mod[layer], 6, axis=-1)
        kind, j = layer % 3, layer // 3
        mix_dtype = F32 if kind == 2 else BF16
        h2 = _norm_mod(x2, P["norm_g"][layer, 0], sc1, sh1, S, mix_dtype)
        if kind == 0:
            x2 = _mixer_rglru(h2, B, S, P["lru"][j], x2, gt1)
        elif kind == 1:
            x2 = _mixer_hgrn2(h2, B, S, P["hg"][j], P["lbs"][layer:layer + 1], x2, gt1)
        else:
            x2 = _mixer_rwkv7(h2, B, S, P["r7"][j], x2, gt1)
        h2, probs_t = _norm_mod(x2, P["norm_g"][layer, 1], sc2, sh2, S, BF16,
                                router_wt=P["moe"][layer]["router_t"])
        moe = _ec_moe(h2, probs_t, P["moe"][layer])
        x2 = x2 + jnp.repeat(gt2, S, axis=0) * moe
    zeros = jnp.zeros((B, D), F32)
    y = _norm_mod(x2, P["final_g"], zeros, zeros, S, F32)
    return y.reshape(B, S, D)


def _pad_cols(w, width):
    return jnp.pad(w, ((0, 0), (0, width - w.shape[1])))


def _pad_rows(w, height):
    return jnp.pad(w, ((0, height - w.shape[0]), (0, 0)))


def kernel(x_prompt, x_sample, c_prompt, c_sample, ada_w, ada_b, norm_g, final_g, lru_w_in, lru_conv_w, lru_conv_b, lru_gate_w, lru_gate_b, lru_lam, lru_w_out, hg_w_in, hg_lb, hg_gn_g, hg_w_out, r7_mu, r7_w_rkv, r7_w0, r7_w1, r7_w2, r7_a0, r7_a1, r7_a2, r7_g1, r7_g2, r7_k_k, r7_k_a, r7_r_k, r7_lnx_g, r7_lnx_b, r7_w_o, moe_router, moe_w_gate, moe_w_up, moe_w_down):
    L = ada_w.shape[0]
    bf = lambda t: t.astype(BF16)

    lower = jnp.cumsum(jax.nn.softmax(hg_lb.astype(F32), axis=0), axis=0)
    P = {"norm_g": norm_g, "final_g": final_g, "lbs": lower - lower[:1]}
    P["lru"] = [dict(w_in=bf(lru_w_in[j]), conv_w=lru_conv_w[j], conv_b=lru_conv_b[j],
                     gw=bf(lru_gate_w[j]), gb=lru_gate_b[j], sp=jax.nn.softplus(-lru_lam[j]),
                     w_out=bf(lru_w_out[j])) for j in range(lru_w_in.shape[0])]
    P["hg"] = [dict(w_in=bf(hg_w_in[j]), gn_g=hg_gn_g[j].reshape(1, -1), w_out=bf(hg_w_out[j]))
               for j in range(hg_w_in.shape[0])]
    P["r7"] = []
    for j in range(r7_mu.shape[0]):
        R = r7_w1.shape[-1]
        Rp = -(-R // LANE) * LANE
        P["r7"].append(dict(
            mu=r7_mu[j], w_r=bf(r7_w_rkv[j, 0]), w_k=bf(r7_w_rkv[j, 1]), w_v=bf(r7_w_rkv[j, 2]),
            w0=r7_w0[j], a0=r7_a0[j],
            w1=bf(jnp.concatenate([_pad_cols(r7_w1[j, d], Rp) for d in range(2)], axis=1)),
            a1=bf(jnp.concatenate([_pad_cols(r7_a1[j, d], Rp) for d in range(2)], axis=1)),
            w2=bf(jnp.stack([_pad_rows(r7_w2[j, d], Rp) for d in range(2)])),
            a2=bf(jnp.stack([_pad_rows(r7_a2[j, d], Rp) for d in range(2)])),
            g1=bf(r7_g1[j]), g2=bf(r7_g2[j]), k_k=r7_k_k[j], k_a=r7_k_a[j], r_k=r7_r_k[j],
            lnx_g=r7_lnx_g[j], lnx_b=r7_lnx_b[j], w_o=bf(r7_w_o[j])))
    P["moe"] = [dict(router_t=moe_router[l].T, w_gate=bf(moe_w_gate[l]), w_up=bf(moe_w_up[l]),
                     w_down=bf(moe_w_down[l])) for l in range(L)]

    Bp, Bs = c_prompt.shape[0], c_sample.shape[0]
    c_all = jnp.concatenate([c_prompt, c_sample], axis=0)
    pad = (-c_all.shape[0]) % SUBLANE
    c_all = jnp.pad(c_all, ((0, pad), (0, 0)))
    mod = _ada_mod(c_all, ada_w, ada_b)
    y_prompt = _trunk(x_prompt, mod[:, :Bp], P)
    y_sample = _trunk(x_sample, mod[:, Bp:Bp + Bs], P)
    return (y_prompt, y_sample)
```

```python
import functools

import jax
import jax.numpy as jnp
from jax import lax
from jax.experimental import pallas as pl
from jax.experimental.pallas import tpu as pltpu

F32 = jnp.float32
BF16 = jnp.bfloat16

NORM_EPS = 1e-6
LNX_EPS = 64e-5
LRU_C = 8.0
CAPACITY_FACTOR = 2
HG_CHUNK = 32
EXP_CLAMP = 80.0

LANE = 128
SUBLANE = 8
VMEM_LIMIT = 48 * 1024 * 1024


def _cparams(sem):
    return pltpu.CompilerParams(dimension_semantics=sem, vmem_limit_bytes=VMEM_LIMIT)


def _pick(n, cands):
    for c in cands:
        if n % c == 0:
            return c
    return n


def _sigmoid(x):
    return 1.0 / (1.0 + jnp.exp(-x))


def _silu(x):
    return x * _sigmoid(x)


def _gelu_tanh(x):
    return 0.5 * x * (1.0 + jnp.tanh(0.7978845608028654 * (x + 0.044715 * (x * x * x))))


def _dot_exact_lhs(a_bf16, x):
    x1 = x.astype(BF16)
    r1 = x - x1.astype(F32)
    x2 = r1.astype(BF16)
    x3 = (r1 - x2.astype(F32)).astype(BF16)
    d = functools.partial(jnp.dot, preferred_element_type=F32)
    return d(a_bf16, x1) + d(a_bf16, x2) + d(a_bf16, x3)


def _ada_body(c_ref, w_ref, b_ref, o_ref):
    x = _silu(c_ref[...])
    acc = jnp.dot(x, w_ref[0], preferred_element_type=F32, precision=lax.Precision.HIGHEST)
    o_ref[0] = acc + b_ref[0]


def _ada_mod(c_all, ada_w, ada_b):
    L, D, N = ada_w.shape
    Mp = c_all.shape[0]
    tn = _pick(N, (1024, 512, 256, 128))
    return pl.pallas_call(
        _ada_body,
        grid=(L, N // tn),
        in_specs=[
            pl.BlockSpec((Mp, D), lambda l, j: (0, 0)),
            pl.BlockSpec((1, D, tn), lambda l, j: (l, 0, j)),
            pl.BlockSpec((1, 1, tn), lambda l, j: (l, 0, j)),
        ],
        out_specs=pl.BlockSpec((1, Mp, tn), lambda l, j: (l, 0, j)),
        out_shape=jax.ShapeDtypeStruct((L, Mp, N), F32),
        compiler_params=_cparams(("parallel", "parallel")),
        name="ada_mod",
    )(c_all, ada_w, ada_b.reshape(L, 1, N))


def _norm_body(x_ref, g_ref, sc_ref, sh_ref, *rest, with_router):
    x = x_ref[...]
    ms = jnp.mean(x * x, axis=-1, keepdims=True)
    y = x * lax.rsqrt(ms + NORM_EPS) * g_ref[...]
    h = y * (1.0 + sc_ref[0]) + sh_ref[0]
    if with_router:
        rwt_ref, o_ref, p_ref = rest
        logits = lax.dot_general(rwt_ref[...], h, (((1,), (1,)), ((), ())),
                                 preferred_element_type=F32, precision=lax.Precision.HIGHEST)
        m = jnp.max(logits, axis=0, keepdims=True)
        e = jnp.exp(logits - m)
        p_ref[...] = e / jnp.sum(e, axis=0, keepdims=True)
    else:
        (o_ref,) = rest
    o_ref[...] = h.astype(o_ref.dtype)


def _norm_mod(x2, g, sc, sh, S, out_dtype, router_wt=None):
    T, D = x2.shape
    B = sc.shape[0]
    tm = _pick(S, (512, 256, 128, 64, 32, 16, 8))
    bmap = lambda i: ((i * tm) // S, 0, 0)
    in_specs = [
        pl.BlockSpec((tm, D), lambda i: (i, 0)),
        pl.BlockSpec((1, D), lambda i: (0, 0)),
        pl.BlockSpec((1, 1, D), bmap),
        pl.BlockSpec((1, 1, D), bmap),
    ]
    args = [x2, g.reshape(1, D), sc.reshape(B, 1, D), sh.reshape(B, 1, D)]
    out_specs = pl.BlockSpec((tm, D), lambda i: (i, 0))
    out_shape = jax.ShapeDtypeStruct((T, D), out_dtype)
    if router_wt is not None:
        E = router_wt.shape[0]
        in_specs.append(pl.BlockSpec((E, D), lambda i: (0, 0)))
        args.append(router_wt)
        out_specs = (out_specs, pl.BlockSpec((E, tm), lambda i: (0, i)))
        out_shape = (out_shape, jax.ShapeDtypeStruct((E, T), F32))
    return pl.pallas_call(
        functools.partial(_norm_body, with_router=router_wt is not None),
        grid=(T // tm,),
        in_specs=in_specs,
        out_specs=out_specs,
        out_shape=out_shape,
        compiler_params=_cparams(("parallel",)),
        name="norm_mod",
    )(*args)


def _mm_body(*refs, act, has_mix, has_res):
    it = iter(refs)
    x_ref = next(it)
    x = x_ref[...]
    if has_mix:
        xx_ref = next(it)
        mu_ref = next(it)
        x = x.astype(F32) + xx_ref[...].astype(F32) * mu_ref[...]
    w_ref = next(it)
    if act == "tanh":
        x = jnp.tanh(x.astype(F32))
    elif act == "sigmoid":
        x = _sigmoid(x.astype(F32))
    acc = jnp.dot(x.astype(BF16), w_ref[...], preferred_element_type=F32)
    if has_res:
        res_ref = next(it)
        gt_ref = next(it)
        acc = res_ref[...] + gt_ref[0] * acc
    o_ref = next(it)
    o_ref[...] = acc.astype(o_ref.dtype)


def _matmul(x, w, *, out_dtype=F32, act=None, mix=None, res=None, gate=None, S=None,
            x_col_block=0, tm=None, tn=None):
    M = x.shape[0]
    K, N = w.shape
    tm = tm or _pick(M if S is None else S, (512, 256, 128, 64, 32, 16, 8))
    tn = tn or _pick(N, (1024, 512, 256, 128))
    cb = x_col_block
    in_specs = [pl.BlockSpec((tm, K), lambda j, i: (i, cb))]
    args = [x]
    if mix is not None:
        xx, mu = mix
        in_specs += [pl.BlockSpec((tm, K), lambda j, i: (i, 0)),
                     pl.BlockSpec((1, K), lambda j, i: (0, 0))]
        args += [xx, mu.reshape(1, K)]
    in_specs.append(pl.BlockSpec((K, tn), lambda j, i: (0, j)))
    args.append(w)
    if res is not None:
        B = gate.shape[0]
        in_specs += [pl.BlockSpec((tm, tn), lambda j, i: (i, j)),
                     pl.BlockSpec((1, 1, tn), lambda j, i: ((i * tm) // S, 0, j))]
        args += [res, gate.reshape(B, 1, N)]
    return pl.pallas_call(
        functools.partial(_mm_body, act=act, has_mix=mix is not None, has_res=res is not None),
        grid=(N // tn, M // tm),
        in_specs=in_specs,
        out_specs=pl.BlockSpec((tm, tn), lambda j, i: (i, j)),
        out_shape=jax.ShapeDtypeStruct((M, N), out_dtype),
        compiler_params=_cparams(("parallel", "parallel")),
        name="matmul",
    )(*args)


def _tile_scan(a, u, reverse):
    R = a.shape[0]
    r8 = lax.broadcasted_iota(jnp.int32, a.shape, 0) & (SUBLANE - 1)
    for d in (1, 2, 4):
        if reverse:
            a_sh = pltpu.roll(a, R - d, axis=0)
            u_sh = pltpu.roll(u, R - d, axis=0)
            m = r8 < SUBLANE - d
        else:
            a_sh = pltpu.roll(a, d, axis=0)
            u_sh = pltpu.roll(u, d, axis=0)
            m = r8 >= d
        u = jnp.where(m, a * u_sh + u, u)
        a = jnp.where(m, a * a_sh, a)
    return a, u


def _rglru_body(*refs, reverse, Tc, nb, bs):
    if reverse:
        (xr_ref, pv_ref, nx_ref, cw_ref, cb_ref, gw_ref, gb_ref, sp_ref,
         hf_ref, gate_ref, o_ref, carry_ref) = refs
    else:
        (xr_ref, pv_ref, nx_ref, cw_ref, cb_ref, gw_ref, gb_ref, sp_ref,
         o_ref, carry_ref) = refs
    c = pl.program_id(2)
    nc = pl.num_programs(2)
    tpos = nc - 1 - c if reverse else c

    @pl.when(c == 0)
    def _():
        carry_ref[...] = jnp.zeros_like(carry_ref)

    x = xr_ref[0]
    pv = jnp.where(tpos > 0, pv_ref[0], 0.0)
    nx = jnp.where(tpos < nc - 1, nx_ref[0], 0.0)
    xp = jnp.concatenate([pv, x, nx], axis=0)
    cw = cw_ref[...]
    xc = cb_ref[...]
    for j in range(4):
        xc = xc + xp[SUBLANE - 2 + j:SUBLANE - 2 + j + Tc] * cw[j:j + 1]
    xcb = xc.astype(BF16)
    rs, gs = [], []
    for j in range(nb):
        blk = xcb[:, j * bs:(j + 1) * bs]
        rs.append(jnp.dot(blk, gw_ref[0, j], preferred_element_type=F32))
        gs.append(jnp.dot(blk, gw_ref[1, j], preferred_element_type=F32))
    r = _sigmoid(jnp.concatenate(rs, axis=1) + gb_ref[0:1])
    ig = _sigmoid(jnp.concatenate(gs, axis=1) + gb_ref[1:2])
    log_a = (-LRU_C) * r * sp_ref[...]
    a = jnp.exp(log_a)
    u = jnp.sqrt(-jnp.tanh(log_a) * (a * a + 1.0)) * (ig * xc)
    a_loc, u_loc = _tile_scan(a, u, reverse)

    h_prev = carry_ref[0:1]
    n_tiles = Tc // SUBLANE
    order = range(n_tiles - 1, -1, -1) if reverse else range(n_tiles)
    for k in order:
        sl = slice(k * SUBLANE, (k + 1) * SUBLANE)
        ht = u_loc[sl] + a_loc[sl] * h_prev
        h_prev = ht[0:1] if reverse else ht[SUBLANE - 1:SUBLANE]
        if reverse:
            y = (hf_ref[0, sl, :] + ht) * _gelu_tanh(gate_ref[0, sl, :])
            o_ref[0, sl, :] = y.astype(o_ref.dtype)
        else:
            o_ref[0, sl, :] = ht
    carry_ref[0:1] = h_prev


def _rglru_dir(u3, conv_w, conv_b, gw, gb, sp, reverse, hf=None):
    B, S, two_dr = u3.shape
    Dr = two_dr // 2
    bs = gw.shape[-1]
    C = _pick(Dr, (512, 256, 128))
    Tc = _pick(S, (256, 128, 64, 32, 16, 8))
    nb = C // bs
    ncb = Dr // C
    nc = S // Tc
    t8 = Tc // SUBLANE
    n8 = S // SUBLANE

    def tmap(c):
        return nc - 1 - c if reverse else c

    in_specs = [
        pl.BlockSpec((1, Tc, C), lambda b, n, c: (b, tmap(c), ncb + n)),
        pl.BlockSpec((1, SUBLANE, C), lambda b, n, c: (b, jnp.maximum(tmap(c) * t8 - 1, 0), ncb + n)),
        pl.BlockSpec((1, SUBLANE, C), lambda b, n, c: (b, jnp.minimum((tmap(c) + 1) * t8, n8 - 1), ncb + n)),
        pl.BlockSpec((conv_w.shape[0], C), lambda b, n, c: (0, n)),
        pl.BlockSpec((1, C), lambda b, n, c: (0, n)),
        pl.BlockSpec((2, nb, bs, bs), lambda b, n, c: (0, n, 0, 0)),
        pl.BlockSpec((2, C), lambda b, n, c: (0, n)),
        pl.BlockSpec((1, C), lambda b, n, c: (0, n)),
    ]
    args = [u3, u3, u3, conv_w, conv_b.reshape(1, Dr), gw, gb, sp]
    if reverse:
        in_specs += [pl.BlockSpec((1, Tc, C), lambda b, n, c: (b, tmap(c), n)),
                     pl.BlockSpec((1, Tc, C), lambda b, n, c: (b, tmap(c), n))]
        args += [hf, u3]
        out_dtype = BF16
    else:
        out_dtype = F32
    return pl.pallas_call(
        functools.partial(_rglru_body, reverse=reverse, Tc=Tc, nb=nb, bs=bs),
        grid=(B, ncb, nc),
        in_specs=in_specs,
        out_specs=pl.BlockSpec((1, Tc, C), lambda b, n, c: (b, tmap(c), n)),
        out_shape=jax.ShapeDtypeStruct((B, S, Dr), out_dtype),
        scratch_shapes=[pltpu.VMEM((SUBLANE, C), F32)],
        compiler_params=_cparams(("parallel", "parallel", "arbitrary")),
        name="rglru_bwd" if reverse else "rglru_fwd",
    )(*args)


def _mixer_rglru(h2, B, S, p, x2, gt):
    u = _matmul(h2, p["w_in"], S=S)
    Dr = u.shape[1] // 2
    u3 = u.reshape(B, S, 2 * Dr)
    hf = _rglru_dir(u3, p["conv_w"], p["conv_b"], p["gw"][0], p["gb"][0], p["sp"][0:1], False)
    y = _rglru_dir(u3, p["conv_w"], p["conv_b"], p["gw"][1], p["gb"][1], p["sp"][1:2], True, hf=hf)
    return _matmul(y.reshape(B * S, Dr), p["w_out"], res=x2, gate=gt, S=S)


def _hgrn2_body(*refs, reverse, Tc, C, HB, K, V):
    if reverse:
        q_ref, z_ref, v_ref, lb_ref, of_ref, g_ref, gn_ref, o_ref, st_ref = refs
    else:
        q_ref, z_ref, v_ref, lb_ref, o_ref, st_ref = refs
    c = pl.program_id(2)

    @pl.when(c == 0)
    def _():
        st_ref[...] = jnp.zeros_like(st_ref)

    dn_t = (((1,), (1,)), ((), ()))
    dn_c = (((0,), (0,)), ((), ()))
    row = lax.broadcasted_iota(jnp.int32, (C, C), 0)
    col = lax.broadcasted_iota(jnp.int32, (C, C), 1)
    keep = (col >= row) if reverse else (col <= row)
    rowt = lax.broadcasted_iota(jnp.int32, (Tc, Tc), 0)
    colt = lax.broadcasted_iota(jnp.int32, (Tc, Tc), 1)
    causal = (colt >= rowt) if reverse else (colt <= rowt)
    tri = jnp.where(causal & ((rowt // C) == (colt // C)), 1.0, 0.0).astype(BF16)
    mid = C // 2
    nsub = Tc // C
    order = list(range(nsub - 1, -1, -1) if reverse else range(nsub))
    heads = range(HB)
    ks = [slice(h * K, (h + 1) * K) for h in heads]
    vs = [slice(h * V, (h + 1) * V) for h in heads]

    lb = lb_ref[...]
    f = lb + (1.0 - lb) * _sigmoid(z_ref[0])
    kk = 1.0 - f
    q = q_ref[0]
    vb = v_ref[0].astype(BF16)
    b = _dot_exact_lhs(tri, jnp.log(f))
    qt, kt, qe, kh, gend = {}, {}, {}, {}, {}
    for j in order:
        sl = slice(j * C, (j + 1) * C)
        bj = b[sl]
        bm = bj[mid:mid + 1]
        b_end = bj[0:1] if reverse else bj[C - 1:C]
        qt[j] = (q[sl] * jnp.exp(jnp.minimum(bj - bm, EXP_CLAMP))).astype(BF16)
        kt[j] = (kk[sl] * jnp.exp(jnp.minimum(bm - bj, EXP_CLAMP))).astype(BF16)
        qe[j] = (q[sl] * jnp.exp(bj)).astype(BF16)
        kh[j] = (kk[sl] * jnp.exp(b_end - bj)).astype(BF16)
        gend[j] = jnp.exp(b_end)
    sc = {(j, h): lax.dot_general(qt[j][:, ks[h]], kt[j][:, ks[h]], dn_t, preferred_element_type=F32)
          for j in order for h in heads}
    dst = {(j, h): lax.dot_general(vb[j * C:(j + 1) * C, vs[h]], kh[j][:, ks[h]], dn_c,
                                   preferred_element_type=F32)
           for j in order for h in heads}
    scm = {jh: jnp.where(keep, sc[jh], 0.0).astype(BF16) for jh in sc}
    ov = {(j, h): jnp.dot(scm[(j, h)], vb[j * C:(j + 1) * C, vs[h]], preferred_element_type=F32)
          for j in order for h in heads}

    st = [st_ref[h] for h in heads]
    for j in order:
        sl = slice(j * C, (j + 1) * C)
        outs = []
        for h in heads:
            oi = lax.dot_general(qe[j][:, ks[h]], st[h].astype(BF16), dn_t, preferred_element_type=F32)
            st[h] = st[h] * gend[j][:, ks[h]] + dst[(j, h)]
            o = oi + ov[(j, h)]
            if reverse:
                ot = of_ref[0, sl, vs[h]] + o
                ms = jnp.mean(ot * ot, axis=-1, keepdims=True)
                o = ot * lax.rsqrt(ms + NORM_EPS) * gn_ref[...]
            outs.append(o)
        oall = outs[0] if HB == 1 else jnp.concatenate(outs, axis=1)
        if reverse:
            o_ref[0, sl, :] = (oall * _silu(g_ref[0, sl, :])).astype(o_ref.dtype)
        else:
            o_ref[0, sl, :] = oall
    for h in heads:
        st_ref[h] = st[h]


def _hgrn2_dir(u3, lb, gn_g, H, K, V, reverse, of=None):
    B, S, _ = u3.shape
    Tc = _pick(S, (256, 128, 64, 32))
    C = min(HG_CHUNK, Tc)
    HB = _pick(H, (2, 1))
    nc = S // Tc
    nh = H // HB
    zoff = (2 if reverse else 1) * nh
    voff = (3 * H * K) // (HB * V)

    def tmap(c):
        return nc - 1 - c if reverse else c

    in_specs = [
        pl.BlockSpec((1, Tc, HB * K), lambda b, h, c: (b, tmap(c), h)),
        pl.BlockSpec((1, Tc, HB * K), lambda b, h, c: (b, tmap(c), zoff + h)),
        pl.BlockSpec((1, Tc, HB * V), lambda b, h, c: (b, tmap(c), voff + h)),
        pl.BlockSpec((1, HB * K), lambda b, h, c: (0, h)),
    ]
    args = [u3, u3, u3, lb]
    if reverse:
        in_specs += [pl.BlockSpec((1, Tc, HB * V), lambda b, h, c: (b, tmap(c), h)),
                     pl.BlockSpec((1, Tc, HB * V), lambda b, h, c: (b, tmap(c), voff + nh + h)),
                     pl.BlockSpec((1, V), lambda b, h, c: (0, 0))]
        args += [of, u3, gn_g]
        out_dtype = BF16
    else:
        out_dtype = F32
    return pl.pallas_call(
        functools.partial(_hgrn2_body, reverse=reverse, Tc=Tc, C=C, HB=HB, K=K, V=V),
        grid=(B, nh, nc),
        in_specs=in_specs,
        out_specs=pl.BlockSpec((1, Tc, HB * V), lambda b, h, c: (b, tmap(c), h)),
        out_shape=jax.ShapeDtypeStruct((B, S, H * V), out_dtype),
        scratch_shapes=[pltpu.VMEM((HB, V, K), F32)],
        compiler_params=_cparams(("parallel", "parallel", "arbitrary")),
        name="hgrn2_bwd" if reverse else "hgrn2_fwd",
    )(*args)


def _mixer_hgrn2(h2, B, S, p, lb, x2, gt):
    D = h2.shape[1]
    V = p["gn_g"].shape[-1]
    H = D // V
    K = lb.shape[-1] // H
    u = _matmul(h2, p["w_in"], S=S)
    u3 = u.reshape(B, S, u.shape[1])
    of = _hgrn2_dir(u3, lb, p["gn_g"], H, K, V, False)
    y = _hgrn2_dir(u3, lb, p["gn_g"], H, K, V, True, of=of)
    return _matmul(y.reshape(B * S, D), p["w_out"], res=x2, gate=gt, S=S)


def _rwkv_scan_body(w_ref, nkk_ref, kka_ref, kd_ref, r_ref, v_ref, y_ref, st_ref, *, Tc, N, reverse):
    c = pl.program_id(1)

    @pl.when(c == 0)
    def _():
        st_ref[...] = jnp.zeros_like(st_ref)

    def step(i, carry):
        t = Tc - 1 - i if reverse else i
        vv = v_ref[t]
        sas = [jnp.zeros_like(vv), jnp.zeros_like(vv)]
        for k in range(N):
            sas[k % 2] = sas[k % 2] + st_ref[k] * nkk_ref[t, k:k + 1, :]
        sa = sas[0] + sas[1]
        ys = [jnp.zeros_like(vv), jnp.zeros_like(vv)]
        for k in range(N):
            s_new = (st_ref[k] * w_ref[t, k:k + 1, :] + sa * kka_ref[t, k:k + 1, :]
                     + vv * kd_ref[t, k:k + 1, :])
            st_ref[k] = s_new
            ys[k % 2] = ys[k % 2] + s_new * r_ref[t, k:k + 1, :]
        y_ref[t] = ys[0] + ys[1]
        return carry

    lax.fori_loop(0, Tc, step, 0)


def _rwkv_scan(w, nkk, kka, kd, r, v, reverse):
    S, N, NC = w.shape
    VH = 2 if NC * 2 == LANE else 1
    NV = N // VH
    ops = [w, nkk, kka, kd, r]
    if VH > 1:
        ops = [jnp.concatenate([t] * VH, axis=-1) for t in ops]
        v = v.reshape(S, VH, NV, NC).transpose(0, 2, 1, 3).reshape(S, NV, VH * NC)
    pad = (-ops[0].shape[-1]) % LANE
    if pad:
        ops = [jnp.pad(t, ((0, 0), (0, 0), (0, pad))) for t in ops]
        v = jnp.pad(v, ((0, 0), (0, 0), (0, pad)))
    NCp = v.shape[-1]
    Tc = _pick(S, (32, 16, 8))
    nc = S // Tc
    tmap = (lambda g, c: (nc - 1 - c, 0, g)) if reverse else (lambda g, c: (c, 0, g))
    y = pl.pallas_call(
        functools.partial(_rwkv_scan_body, Tc=Tc, N=N, reverse=reverse),
        grid=(NCp // LANE, nc),
        in_specs=[pl.BlockSpec((Tc, N, LANE), tmap)] * 5 + [pl.BlockSpec((Tc, NV, LANE), tmap)],
        out_specs=pl.BlockSpec((Tc, NV, LANE), tmap),
        out_shape=jax.ShapeDtypeStruct((S, NV, NCp), F32),
        scratch_shapes=[pltpu.VMEM((N, NV, LANE), F32)],
        compiler_params=_cparams(("parallel", "arbitrary")),
        name="rwkv7_scan_bwd" if reverse else "rwkv7_scan_fwd",
    )(*ops, v)
    if VH > 1:
        y = y[:, :, :VH * NC].reshape(S, NV, VH, NC).transpose(0, 2, 1, 3).reshape(S, N, NC)
    return y[:, :, :NC]


def _mixer_rwkv7(h2, B, S, p, x2, gt):
    T, D = h2.shape
    H, N = p["r_k"].shape
    h3 = h2.reshape(B, S, D)
    zero = jnp.zeros_like(h3[:, :1])
    prev = jnp.concatenate([zero, h3[:, :-1]], axis=1)
    nxt = jnp.concatenate([h3[:, 1:], zero], axis=1)
    xx = (0.5 * (prev + nxt) - h3).reshape(T, D)
    mu = p["mu"]
    mm = lambda n, w, **kw: _matmul(h2, w, mix=(xx, mu[n]), S=S, **kw)
    r = mm(0, p["w_r"])
    k = mm(2, p["w_k"])
    v = mm(3, p["w_v"])
    tw = mm(1, p["w1"])
    ta = mm(4, p["a1"])
    tg = mm(5, p["g1"])
    g = _matmul(tg, p["g2"], act="sigmoid", S=S)
    kk = (k * p["k_k"]).reshape(T, H, N)
    kk = kk / jnp.maximum(jnp.sqrt(jnp.sum(kk * kk, axis=-1, keepdims=True)), 1e-12)
    kk = kk.reshape(T, D)
    rh = r.reshape(T, H, N)
    vh = v.reshape(T, H, N)
    per_dir = []
    bonus = 0.0
    for d in range(2):
        lw = _matmul(tw, p["w2"][d], act="tanh", S=S, x_col_block=d)
        wl = -jax.nn.softplus(-(p["w0"][d] + lw)) - 0.5
        decay = jnp.exp(-jnp.exp(wl))
        la = _matmul(ta, p["a2"][d], S=S, x_col_block=d)
        a = _sigmoid(p["a0"][d] + la)
        kd = k * (1.0 + (a - 1.0) * p["k_a"])
        bonus = bonus + jnp.sum(rh * kd.reshape(T, H, N) * p["r_k"], axis=-1, keepdims=True) * vh
        per_dir.append((decay, kk * a, kd))

    def chains(t2):
        return t2.reshape(B, S, H, N).transpose(1, 3, 0, 2).reshape(S, N, B * H)

    def unchain(t3):
        return t3.reshape(S, N, B, H).transpose(2, 0, 3, 1)

    shared = (chains(-kk), chains(r), chains(v))
    outs = []
    for d in range(2):
        decay, kka, kd = per_dir[d]
        outs.append(unchain(_rwkv_scan(chains(decay), shared[0], chains(kka), chains(kd),
                                       shared[1], shared[2], reverse=(d == 1))))
    yf, yb = outs
    y = (yf + yb).reshape(T, H, N)
    mean = jnp.mean(y, axis=-1, keepdims=True)
    var = jnp.mean(jnp.square(y - mean), axis=-1, keepdims=True)
    yn = ((y - mean) * lax.rsqrt(var + LNX_EPS)).reshape(T, D) * p["lnx_g"] + p["lnx_b"]
    out = ((yn + bonus.reshape(T, D)) * g).astype(BF16)
    return _matmul(out, p["w_o"], res=x2, gate=gt, S=S)


def _ffn_body(x_ref, wg_ref, wu_ref, wd_ref, gt_ref, o_ref):
    f = pl.program_id(2)
    x = x_ref[0]
    g = jnp.dot(x, wg_ref[0], preferred_element_type=F32)
    u = jnp.dot(x, wu_ref[0], preferred_element_type=F32)
    hid = (_silu(g) * u).astype(BF16)
    part = jnp.dot(hid, wd_ref[0], preferred_element_type=F32)

    @pl.when(f == 0)
    def _():
        o_ref[0] = part

    @pl.when(f > 0)
    def _():
        o_ref[0] = o_ref[0] + part

    @pl.when(f == pl.num_programs(2) - 1)
    def _():
        o_ref[0] = o_ref[0] * gt_ref[0]


def _moe_ffn(xe, wg, wu, wd, gates):
    E, cap, D = xe.shape
    Fd = wg.shape[2]
    tm = _pick(cap, (512, 256, 128, 64, 32, 16, 8))
    tf = _pick(Fd, (512, 256, 128))
    return pl.pallas_call(
        _ffn_body,
        grid=(E, cap // tm, Fd // tf),
        in_specs=[
            pl.BlockSpec((1, tm, D), lambda e, m, f: (e, m, 0)),
            pl.BlockSpec((1, D, tf), lambda e, m, f: (e, 0, f)),
            pl.BlockSpec((1, D, tf), lambda e, m, f: (e, 0, f)),
            pl.BlockSpec((1, tf, D), lambda e, m, f: (e, f, 0)),
            pl.BlockSpec((1, tm, 1), lambda e, m, f: (e, m, 0)),
        ],
        out_specs=pl.BlockSpec((1, tm, D), lambda e, m, f: (e, m, 0)),
        out_shape=jax.ShapeDtypeStruct((E, cap, D), F32),
        compiler_params=_cparams(("parallel", "parallel", "arbitrary")),
        name="moe_ffn",
    )(xe, wg, wu, wd, gates.reshape(E, cap, 1))


def _ec_moe(h2, probs_t, p):
    T, D = h2.shape
    E = probs_t.shape[0]
    cap = CAPACITY_FACTOR * T // E
    gates, idx = lax.top_k(probs_t, cap)
    xe = h2[idx]
    ye = _moe_ffn(xe, p["w_gate"], p["w_up"], p["w_down"], gates)
    return jnp.zeros((T, D), F32).at[idx.reshape(-1)].add(ye.reshape(-1, D))


def _trunk(x, mod, P):
    B, S, D = x.shape
    T = B * S
    x2 = x.reshape(T, D)
    L = mod.shape[0]
    for layer in range(L):
        sh1, sc1, gt1, sh2, sc2, gt2 = jnp.split(mod[layer], 6, axis=-1)
        kind, j = layer % 3, layer // 3
        mix_dtype = F32 if kind == 2 else BF16
        h2 = _norm_mod(x2, P["norm_g"][layer, 0], sc1, sh1, S, mix_dtype)
        if kind == 0:
            x2 = _mixer_rglru(h2, B, S, P["lru"][j], x2, gt1)
        elif kind == 1:
            x2 = _mixer_hgrn2(h2, B, S, P["hg"][j], P["lbs"][layer:layer + 1], x2, gt1)
        else:
            x2 = _mixer_rwkv7(h2, B, S, P["r7"][j], x2, gt1)
        h2, probs_t = _norm_mod(x2, P["norm_g"][layer, 1], sc2, sh2, S, BF16,
                                router_wt=P["moe"][layer]["router_t"])
        moe = _ec_moe(h2, probs_t, P["moe"][layer])
        x2 = x2 + jnp.repeat(gt2, S, axis=0) * moe
    zeros = jnp.zeros((B, D), F32)
    y = _norm_mod(x2, P["final_g"], zeros, zeros, S, F32)
    return y.reshape(B, S, D)


def _pad_cols(w, width):
    return jnp.pad(w, ((0, 0), (0, width - w.shape[1])))


def _pad_rows(w, height):
    return jnp.pad(w, ((0, height - w.shape[0]), (0, 0)))


def kernel(x_prompt, x_sample, c_prompt, c_sample, ada_w, ada_b, norm_g, final_g, lru_w_in, lru_conv_w, lru_conv_b, lru_gate_w, lru_gate_b, lru_lam, lru_w_out, hg_w_in, hg_lb, hg_gn_g, hg_w_out, r7_mu, r7_w_rkv, r7_w0, r7_w1, r7_w2, r7_a0, r7_a1, r7_a2, r7_g1, r7_g2, r7_k_k, r7_k_a, r7_r_k, r7_lnx_g, r7_lnx_b, r7_w_o, moe_router, moe_w_gate, moe_w_up, moe_w_down):
    L = ada_w.shape[0]
    bf = lambda t: t.astype(BF16)

    lower = jnp.cumsum(jax.nn.softmax(hg_lb.astype(F32), axis=0), axis=0)
    P = {"norm_g": norm_g, "final_g": final_g, "lbs": lower - lower[:1]}
    P["lru"] = [dict(w_in=bf(lru_w_in[j]), conv_w=lru_conv_w[j], conv_b=lru_conv_b[j],
                     gw=bf(lru_gate_w[j]), gb=lru_gate_b[j], sp=jax.nn.softplus(-lru_lam[j]),
                     w_out=bf(lru_w_out[j])) for j in range(lru_w_in.shape[0])]
    P["hg"] = [dict(w_in=bf(hg_w_in[j]), gn_g=hg_gn_g[j].reshape(1, -1), w_out=bf(hg_w_out[j]))
               for j in range(hg_w_in.shape[0])]
    P["r7"] = []
    for j in range(r7_mu.shape[0]):
        R = r7_w1.shape[-1]
        Rp = -(-R // LANE) * LANE
        P["r7"].append(dict(
            mu=r7_mu[j], w_r=bf(r7_w_rkv[j, 0]), w_k=bf(r7_w_rkv[j, 1]), w_v=bf(r7_w_rkv[j, 2]),
            w0=r7_w0[j], a0=r7_a0[j],
            w1=bf(jnp.concatenate([_pad_cols(r7_w1[j, d], Rp) for d in range(2)], axis=1)),
            a1=bf(jnp.concatenate([_pad_cols(r7_a1[j, d], Rp) for d in range(2)], axis=1)),
            w2=bf(jnp.stack([_pad_rows(r7_w2[j, d], Rp) for d in range(2)])),
            a2=bf(jnp.stack([_pad_rows(r7_a2[j, d], Rp) for d in range(2)])),
            g1=bf(r7_g1[j]), g2=bf(r7_g2[j]), k_k=r7_k_k[j], k_a=r7_k_a[j], r_k=r7_r_k[j],
            lnx_g=r7_lnx_g[j], lnx_b=r7_lnx_b[j], w_o=bf(r7_w_o[j])))
    P["moe"] = [dict(router_t=moe_router[l].T, w_gate=bf(moe_w_gate[l]), w_up=bf(moe_w_up[l]),
                     w_down=bf(moe_w_down[l])) for l in range(L)]

    Bp, Bs = c_prompt.shape[0], c_sample.shape[0]
    c_all = jnp.concatenate([c_prompt, c_sample], axis=0)
    pad = (-c_all.shape[0]) % SUBLANE
    c_all = jnp.pad(c_all, ((0, pad), (0, 0)))
    mod = _ada_mod(c_all, ada_w, ada_b)
    y_prompt = _trunk(x_prompt, mod[:, :Bp], P)
    y_sample = _trunk(x_sample, mod[:, Bp:Bp + Bs], P)
    return (y_prompt, y_sample)
```

```python
import functools

import jax
import jax.numpy as jnp
from jax import lax
from jax.experimental import pallas as pl
from jax.experimental.pallas import tpu as pltpu

F32 = jnp.float32
BF16 = jnp.bfloat16

NORM_EPS = 1e-6
LNX_EPS = 64e-5
LRU_C = 8.0
CAPACITY_FACTOR = 2
HG_CHUNK = 32
EXP_CLAMP = 80.0

LANE = 128
SUBLANE = 8
VMEM_LIMIT = 48 * 1024 * 1024


def _cparams(sem):
    return pltpu.CompilerParams(dimension_semantics=sem, vmem_limit_bytes=VMEM_LIMIT)


def _pick(n, cands):
    for c in cands:
        if n % c == 0:
            return c
    return n


def _sigmoid(x):
    return 1.0 / (1.0 + jnp.exp(-x))


def _silu(x):
    return x * _sigmoid(x)


def _gelu_tanh(x):
    return 0.5 * x * (1.0 + jnp.tanh(0.7978845608028654 * (x + 0.044715 * (x * x * x))))


def _dot_exact_lhs(a_bf16, x):
    x1 = x.astype(BF16)
    r1 = x - x1.astype(F32)
    x2 = r1.astype(BF16)
    x3 = (r1 - x2.astype(F32)).astype(BF16)
    d = functools.partial(jnp.dot, preferred_element_type=F32)
    return d(a_bf16, x1) + d(a_bf16, x2) + d(a_bf16, x3)


def _ada_body(c_ref, w_ref, b_ref, o_ref):
    x = _silu(c_ref[...])
    acc = jnp.dot(x, w_ref[0], preferred_element_type=F32, precision=lax.Precision.HIGHEST)
    o_ref[0] = acc + b_ref[0]


def _ada_mod(c_all, ada_w, ada_b):
    L, D, N = ada_w.shape
    Mp = c_all.shape[0]
    tn = _pick(N, (1024, 512, 256, 128))
    return pl.pallas_call(
        _ada_body,
        grid=(L, N // tn),
        in_specs=[
            pl.BlockSpec((Mp, D), lambda l, j: (0, 0)),
            pl.BlockSpec((1, D, tn), lambda l, j: (l, 0, j)),
            pl.BlockSpec((1, 1, tn), lambda l, j: (l, 0, j)),
        ],
        out_specs=pl.BlockSpec((1, Mp, tn), lambda l, j: (l, 0, j)),
        out_shape=jax.ShapeDtypeStruct((L, Mp, N), F32),
        compiler_params=_cparams(("parallel", "parallel")),
        name="ada_mod",
    )(c_all, ada_w, ada_b.reshape(L, 1, N))


def _norm_body(x_ref, g_ref, sc_ref, sh_ref, *rest, with_router):
    x = x_ref[...]
    ms = jnp.mean(x * x, axis=-1, keepdims=True)
    y = x * lax.rsqrt(ms + NORM_EPS) * g_ref[...]
    h = y * (1.0 + sc_ref[0]) + sh_ref[0]
    if with_router:
        rwt_ref, o_ref, p_ref = rest
        logits = lax.dot_general(rwt_ref[...], h, (((1,), (1,)), ((), ())),
                                 preferred_element_type=F32, precision=lax.Precision.HIGHEST)
        m = jnp.max(logits, axis=0, keepdims=True)
        e = jnp.exp(logits - m)
        p_ref[...] = e / jnp.sum(e, axis=0, keepdims=True)
    else:
        (o_ref,) = rest
    o_ref[...] = h.astype(o_ref.dtype)


def _norm_mod(x2, g, sc, sh, S, out_dtype, router_wt=None):
    T, D = x2.shape
    B = sc.shape[0]
    tm = _pick(S, (512, 256, 128, 64, 32, 16, 8))
    bmap = lambda i: ((i * tm) // S, 0, 0)
    in_specs = [
        pl.BlockSpec((tm, D), lambda i: (i, 0)),
        pl.BlockSpec((1, D), lambda i: (0, 0)),
        pl.BlockSpec((1, 1, D), bmap),
        pl.BlockSpec((1, 1, D), bmap),
    ]
    args = [x2, g.reshape(1, D), sc.reshape(B, 1, D), sh.reshape(B, 1, D)]
    out_specs = pl.BlockSpec((tm, D), lambda i: (i, 0))
    out_shape = jax.ShapeDtypeStruct((T, D), out_dtype)
    if router_wt is not None:
        E = router_wt.shape[0]
        in_specs.append(pl.BlockSpec((E, D), lambda i: (0, 0)))
        args.append(router_wt)
        out_specs = (out_specs, pl.BlockSpec((E, tm), lambda i: (0, i)))
        out_shape = (out_shape, jax.ShapeDtypeStruct((E, T), F32))
    return pl.pallas_call(
        functools.partial(_norm_body, with_router=router_wt is not None),
        grid=(T // tm,),
        in_specs=in_specs,
        out_specs=out_specs,
        out_shape=out_shape,
        compiler_params=_cparams(("parallel",)),
        name="norm_mod",
    )(*args)


def _mm_body(*refs, act, has_mix, has_mul, has_res):
    it = iter(refs)
    x_ref = next(it)
    x = x_ref[...]
    if has_mul:
        x = x * next(it)[...]
    if has_mix:
        xx_ref = next(it)
        mu_ref = next(it)
        x = x.astype(F32) + xx_ref[...].astype(F32) * mu_ref[...]
    w_ref = next(it)
    if act == "tanh":
        x = jnp.tanh(x.astype(F32))
    elif act == "sigmoid":
        x = _sigmoid(x.astype(F32))
    acc = jnp.dot(x.astype(BF16), w_ref[...], preferred_element_type=F32)
    if has_res:
        res_ref = next(it)
        gt_ref = next(it)
        acc = res_ref[...] + gt_ref[0] * acc
    o_ref = next(it)
    o_ref[...] = acc.astype(o_ref.dtype)


def _matmul(x, w, *, out_dtype=F32, act=None, mix=None, mul=None, res=None, gate=None, S=None,
            x_col_block=0, tm=None, tn=None):
    M = x.shape[0]
    K, N = w.shape
    tm = tm or _pick(M if S is None else S, (512, 256, 128, 64, 32, 16, 8))
    tn = tn or _pick(N, (1024, 512, 256, 128))
    cb = x_col_block
    in_specs = [pl.BlockSpec((tm, K), lambda j, i: (i, cb))]
    args = [x]
    if mul is not None:
        in_specs.append(pl.BlockSpec((tm, K), lambda j, i: (i, 0)))
        args.append(mul)
    if mix is not None:
        xx, mu = mix
        in_specs += [pl.BlockSpec((tm, K), lambda j, i: (i, 0)),
                     pl.BlockSpec((1, K), lambda j, i: (0, 0))]
        args += [xx, mu.reshape(1, K)]
    in_specs.append(pl.BlockSpec((K, tn), lambda j, i: (0, j)))
    args.append(w)
    if res is not None:
        B = gate.shape[0]
        in_specs += [pl.BlockSpec((tm, tn), lambda j, i: (i, j)),
                     pl.BlockSpec((1, 1, tn), lambda j, i: ((i * tm) // S, 0, j))]
        args += [res, gate.reshape(B, 1, N)]
    return pl.pallas_call(
        functools.partial(_mm_body, act=act, has_mix=mix is not None, has_mul=mul is not None,
                          has_res=res is not None),
        grid=(N // tn, M // tm),
        in_specs=in_specs,
        out_specs=pl.BlockSpec((tm, tn), lambda j, i: (i, j)),
        out_shape=jax.ShapeDtypeStruct((M, N), out_dtype),
        compiler_params=_cparams(("parallel", "parallel")),
        name="matmul",
    )(*args)


def _tile_scan(a, u, reverse):
    R = a.shape[0]
    r8 = lax.broadcasted_iota(jnp.int32, a.shape, 0) & (SUBLANE - 1)
    for d in (1, 2, 4):
        if reverse:
            a_sh = pltpu.roll(a, R - d, axis=0)
            u_sh = pltpu.roll(u, R - d, axis=0)
            m = r8 < SUBLANE - d
        else:
            a_sh = pltpu.roll(a, d, axis=0)
            u_sh = pltpu.roll(u, d, axis=0)
            m = r8 >= d
        u = jnp.where(m, a * u_sh + u, u)
        a = jnp.where(m, a * a_sh, a)
    return a, u


def _rglru_body(*refs, reverse, Tc, nb, bs):
    if reverse:
        (xr_ref, pv_ref, nx_ref, cw_ref, cb_ref, gw_ref, gb_ref, sp_ref,
         hf_ref, gate_ref, o_ref, carry_ref) = refs
    else:
        (xr_ref, pv_ref, nx_ref, cw_ref, cb_ref, gw_ref, gb_ref, sp_ref,
         o_ref, carry_ref) = refs
    c = pl.program_id(2)
    nc = pl.num_programs(2)
    tpos = nc - 1 - c if reverse else c

    @pl.when(c == 0)
    def _():
        carry_ref[...] = jnp.zeros_like(carry_ref)

    x = xr_ref[0]
    pv = jnp.where(tpos > 0, pv_ref[0], 0.0)
    nx = jnp.where(tpos < nc - 1, nx_ref[0], 0.0)
    xp = jnp.concatenate([pv, x, nx], axis=0)
    cw = cw_ref[...]
    xc = cb_ref[...]
    for j in range(4):
        xc = xc + xp[SUBLANE - 2 + j:SUBLANE - 2 + j + Tc] * cw[j:j + 1]
    xcb = xc.astype(BF16)
    rs, gs = [], []
    for j in range(nb):
        blk = xcb[:, j * bs:(j + 1) * bs]
        rs.append(jnp.dot(blk, gw_ref[0, j], preferred_element_type=F32))
        gs.append(jnp.dot(blk, gw_ref[1, j], preferred_element_type=F32))
    r = _sigmoid(jnp.concatenate(rs, axis=1) + gb_ref[0:1])
    ig = _sigmoid(jnp.concatenate(gs, axis=1) + gb_ref[1:2])
    log_a = (-LRU_C) * r * sp_ref[...]
    a = jnp.exp(log_a)
    u = jnp.sqrt(-jnp.tanh(log_a) * (a * a + 1.0)) * (ig * xc)
    a_loc, u_loc = _tile_scan(a, u, reverse)

    h_prev = carry_ref[0:1]
    n_tiles = Tc // SUBLANE
    order = range(n_tiles - 1, -1, -1) if reverse else range(n_tiles)
    for k in order:
        sl = slice(k * SUBLANE, (k + 1) * SUBLANE)
        ht = u_loc[sl] + a_loc[sl] * h_prev
        h_prev = ht[0:1] if reverse else ht[SUBLANE - 1:SUBLANE]
        if reverse:
            y = (hf_ref[0, sl, :] + ht) * _gelu_tanh(gate_ref[0, sl, :])
            o_ref[0, sl, :] = y.astype(o_ref.dtype)
        else:
            o_ref[0, sl, :] = ht
    carry_ref[0:1] = h_prev


def _rglru_dir(u3, conv_w, conv_b, gw, gb, sp, reverse, hf=None):
    B, S, two_dr = u3.shape
    Dr = two_dr // 2
    bs = gw.shape[-1]
    C = _pick(Dr, (512, 256, 128))
    Tc = _pick(S, (256, 128, 64, 32, 16, 8))
    nb = C // bs
    ncb = Dr // C
    nc = S // Tc
    t8 = Tc // SUBLANE
    n8 = S // SUBLANE

    def tmap(c):
        return nc - 1 - c if reverse else c

    in_specs = [
        pl.BlockSpec((1, Tc, C), lambda b, n, c: (b, tmap(c), ncb + n)),
        pl.BlockSpec((1, SUBLANE, C), lambda b, n, c: (b, jnp.maximum(tmap(c) * t8 - 1, 0), ncb + n)),
        pl.BlockSpec((1, SUBLANE, C), lambda b, n, c: (b, jnp.minimum((tmap(c) + 1) * t8, n8 - 1), ncb + n)),
        pl.BlockSpec((conv_w.shape[0], C), lambda b, n, c: (0, n)),
        pl.BlockSpec((1, C), lambda b, n, c: (0, n)),
        pl.BlockSpec((2, nb, bs, bs), lambda b, n, c: (0, n, 0, 0)),
        pl.BlockSpec((2, C), lambda b, n, c: (0, n)),
        pl.BlockSpec((1, C), lambda b, n, c: (0, n)),
    ]
    args = [u3, u3, u3, conv_w, conv_b.reshape(1, Dr), gw, gb, sp]
    if reverse:
        in_specs += [pl.BlockSpec((1, Tc, C), lambda b, n, c: (b, tmap(c), n)),
                     pl.BlockSpec((1, Tc, C), lambda b, n, c: (b, tmap(c), n))]
        args += [hf, u3]
        out_dtype = BF16
    else:
        out_dtype = F32
    return pl.pallas_call(
        functools.partial(_rglru_body, reverse=reverse, Tc=Tc, nb=nb, bs=bs),
        grid=(B, ncb, nc),
        in_specs=in_specs,
        out_specs=pl.BlockSpec((1, Tc, C), lambda b, n, c: (b, tmap(c), n)),
        out_shape=jax.ShapeDtypeStruct((B, S, Dr), out_dtype),
        scratch_shapes=[pltpu.VMEM((SUBLANE, C), F32)],
        compiler_params=_cparams(("parallel", "parallel", "arbitrary")),
        name="rglru_bwd" if reverse else "rglru_fwd",
    )(*args)


def _mixer_rglru(h2, B, S, p, x2, gt):
    u = _matmul(h2, p["w_in"], S=S)
    Dr = u.shape[1] // 2
    u3 = u.reshape(B, S, 2 * Dr)
    hf = _rglru_dir(u3, p["conv_w"], p["conv_b"], p["gw"][0], p["gb"][0], p["sp"][0:1], False)
    y = _rglru_dir(u3, p["conv_w"], p["conv_b"], p["gw"][1], p["gb"][1], p["sp"][1:2], True, hf=hf)
    return _matmul(y.reshape(B * S, Dr), p["w_out"], res=x2, gate=gt, S=S)


def _hgrn2_body(*refs, reverse, Tc, C, HB, K, V):
    if reverse:
        q_ref, z_ref, v_ref, lb_ref, of_ref, g_ref, gn_ref, o_ref, st_ref = refs
    else:
        q_ref, z_ref, v_ref, lb_ref, o_ref, st_ref = refs
    c = pl.program_id(2)

    @pl.when(c == 0)
    def _():
        st_ref[...] = jnp.zeros_like(st_ref)

    dn_t = (((1,), (1,)), ((), ()))
    dn_c = (((0,), (0,)), ((), ()))
    row = lax.broadcasted_iota(jnp.int32, (C, C), 0)
    col = lax.broadcasted_iota(jnp.int32, (C, C), 1)
    keep = (col >= row) if reverse else (col <= row)
    rowt = lax.broadcasted_iota(jnp.int32, (Tc, Tc), 0)
    colt = lax.broadcasted_iota(jnp.int32, (Tc, Tc), 1)
    causal = (colt >= rowt) if reverse else (colt <= rowt)
    tri = jnp.where(causal & ((rowt // C) == (colt // C)), 1.0, 0.0).astype(BF16)
    mid = C // 2
    nsub = Tc // C
    order = list(range(nsub - 1, -1, -1) if reverse else range(nsub))
    heads = range(HB)
    ks = [slice(h * K, (h + 1) * K) for h in heads]
    vs = [slice(h * V, (h + 1) * V) for h in heads]

    lb = lb_ref[...]
    f = lb + (1.0 - lb) * _sigmoid(z_ref[0])
    kk = 1.0 - f
    q = q_ref[0]
    vb = v_ref[0].astype(BF16)
    b = _dot_exact_lhs(tri, jnp.log(f))
    qt, kt, qe, kh, gend = {}, {}, {}, {}, {}
    for j in order:
        sl = slice(j * C, (j + 1) * C)
        bj = b[sl]
        bm = bj[mid:mid + 1]
        b_end = bj[0:1] if reverse else bj[C - 1:C]
        qt[j] = (q[sl] * jnp.exp(jnp.minimum(bj - bm, EXP_CLAMP))).astype(BF16)
        kt[j] = (kk[sl] * jnp.exp(jnp.minimum(bm - bj, EXP_CLAMP))).astype(BF16)
        qe[j] = (q[sl] * jnp.exp(bj)).astype(BF16)
        kh[j] = (kk[sl] * jnp.exp(b_end - bj)).astype(BF16)
        gend[j] = jnp.exp(b_end)
    sc = {(j, h): lax.dot_general(qt[j][:, ks[h]], kt[j][:, ks[h]], dn_t, preferred_element_type=F32)
          for j in order for h in heads}
    dst = {(j, h): lax.dot_general(vb[j * C:(j + 1) * C, vs[h]], kh[j][:, ks[h]], dn_c,
                                   preferred_element_type=F32)
           for j in order for h in heads}
    scm = {jh: jnp.where(keep, sc[jh], 0.0).astype(BF16) for jh in sc}
    ov = {(j, h): jnp.dot(scm[(j, h)], vb[j * C:(j + 1) * C, vs[h]], preferred_element_type=F32)
          for j in order for h in heads}

    st = [st_ref[h] for h in heads]
    for j in order:
        sl = slice(j * C, (j + 1) * C)
        outs = []
        for h in heads:
            oi = lax.dot_general(qe[j][:, ks[h]], st[h].astype(BF16), dn_t, preferred_element_type=F32)
            st[h] = st[h] * gend[j][:, ks[h]] + dst[(j, h)]
            o = oi + ov[(j, h)]
            if reverse:
                ot = of_ref[0, sl, vs[h]] + o
                ms = jnp.mean(ot * ot, axis=-1, keepdims=True)
                o = ot * lax.rsqrt(ms + NORM_EPS) * gn_ref[...]
            outs.append(o)
        oall = outs[0] if HB == 1 else jnp.concatenate(outs, axis=1)
        if reverse:
            o_ref[0, sl, :] = (oall * _silu(g_ref[0, sl, :])).astype(o_ref.dtype)
        else:
            o_ref[0, sl, :] = oall
    for h in heads:
        st_ref[h] = st[h]


def _hgrn2_dir(u3, lb, gn_g, H, K, V, reverse, of=None):
    B, S, _ = u3.shape
    Tc = _pick(S, (256, 128, 64, 32))
    C = min(HG_CHUNK, Tc)
    HB = _pick(H, (2, 1))
    nc = S // Tc
    nh = H // HB
    zoff = (2 if reverse else 1) * nh
    voff = (3 * H * K) // (HB * V)

    def tmap(c):
        return nc - 1 - c if reverse else c

    in_specs = [
        pl.BlockSpec((1, Tc, HB * K), lambda b, h, c: (b, tmap(c), h)),
        pl.BlockSpec((1, Tc, HB * K), lambda b, h, c: (b, tmap(c), zoff + h)),
        pl.BlockSpec((1, Tc, HB * V), lambda b, h, c: (b, tmap(c), voff + h)),
        pl.BlockSpec((1, HB * K), lambda b, h, c: (0, h)),
    ]
    args = [u3, u3, u3, lb]
    if reverse:
        in_specs += [pl.BlockSpec((1, Tc, HB * V), lambda b, h, c: (b, tmap(c), h)),
                     pl.BlockSpec((1, Tc, HB * V), lambda b, h, c: (b, tmap(c), voff + nh + h)),
                     pl.BlockSpec((1, V), lambda b, h, c: (0, 0))]
        args += [of, u3, gn_g]
        out_dtype = BF16
    else:
        out_dtype = F32
    return pl.pallas_call(
        functools.partial(_hgrn2_body, reverse=reverse, Tc=Tc, C=C, HB=HB, K=K, V=V),
        grid=(B, nh, nc),
        in_specs=in_specs,
        out_specs=pl.BlockSpec((1, Tc, HB * V), lambda b, h, c: (b, tmap(c), h)),
        out_shape=jax.ShapeDtypeStruct((B, S, H * V), out_dtype),
        scratch_shapes=[pltpu.VMEM((HB, V, K), F32)],
        compiler_params=_cparams(("parallel", "parallel", "arbitrary")),
        name="hgrn2_bwd" if reverse else "hgrn2_fwd",
    )(*args)


def _mixer_hgrn2(h2, B, S, p, lb, x2, gt):
    D = h2.shape[1]
    V = p["gn_g"].shape[-1]
    H = D // V
    K = lb.shape[-1] // H
    u = _matmul(h2, p["w_in"], S=S)
    u3 = u.reshape(B, S, u.shape[1])
    of = _hgrn2_dir(u3, lb, p["gn_g"], H, K, V, False)
    y = _hgrn2_dir(u3, lb, p["gn_g"], H, K, V, True, of=of)
    return _matmul(y.reshape(B * S, D), p["w_out"], res=x2, gate=gt, S=S)


def _rwkv_scan_body(*refs, Tc, N, VH, reverse):
    if reverse:
        (r_ref, k_ref, lw_ref, la_ref, v_ref, w0_ref, a0_ref, kkp_ref, kap_ref, rkp_ref,
         yf_ref, bof_ref, lg_ref, lb_ref, o_ref, st_ref, op_ref, bon_ref) = refs
    else:
        (r_ref, k_ref, lw_ref, la_ref, v_ref, w0_ref, a0_ref, kkp_ref, kap_ref, rkp_ref,
         y_ref, bo_ref, st_ref, op_ref, bon_ref) = refs
    yo_ref = o_ref if reverse else y_ref
    c = pl.program_id(1)

    @pl.when(c == 0)
    def _():
        st_ref[...] = jnp.zeros_like(st_ref)

    def vsum(x):
        s = jnp.sum(x, axis=1)
        for b in range(1, VH):
            s = s + pltpu.roll(s, b * (LANE // VH), axis=1)
        return s[:, None, :]

    x = w0_ref[...] + lw_ref[...]
    wl = -(jnp.maximum(-x, 0.0) + jnp.log1p(jnp.exp(-jnp.abs(x)))) - 0.5
    op_ref[0] = jnp.exp(-jnp.exp(wl))
    a = _sigmoid(a0_ref[...] + la_ref[...])
    kt = k_ref[...]
    kd = kt * (1.0 + (a - 1.0) * kap_ref[...])
    op_ref[3] = kd
    kx = kt * kkp_ref[...]
    nrm = jnp.sqrt(jnp.sum(kx * kx, axis=1, keepdims=True))
    kap = kx / jnp.maximum(nrm, 1e-12)
    op_ref[1] = -kap
    op_ref[2] = kap * a
    bon_ref[...] = jnp.sum(r_ref[...] * kd * rkp_ref[...], axis=1, keepdims=True)

    def step(i, carry):
        t = Tc - 1 - i if reverse else i
        vv = v_ref[t]
        sas = [jnp.zeros_like(vv), jnp.zeros_like(vv)]
        for k in range(N):
            sas[k % 2] = sas[k % 2] + st_ref[k] * op_ref[1, t, k:k + 1, :]
        sa = sas[0] + sas[1]
        ys = [jnp.zeros_like(vv), jnp.zeros_like(vv)]
        for k in range(N):
            s_new = (st_ref[k] * op_ref[0, t, k:k + 1, :] + sa * op_ref[2, t, k:k + 1, :]
                     + vv * op_ref[3, t, k:k + 1, :])
            st_ref[k] = s_new
            ys[k % 2] = ys[k % 2] + s_new * r_ref[t, k:k + 1, :]
        yo_ref[t] = ys[0] + ys[1]
        return carry

    lax.fori_loop(0, Tc, step, 0)

    bonus = bon_ref[...] * v_ref[...]
    if reverse:
        yt = yf_ref[...] + o_ref[...]
        yc = yt - vsum(yt) * (1.0 / N)
        var = vsum(yc * yc) * (1.0 / N)
        o_ref[...] = yc * lax.rsqrt(var + LNX_EPS) * lg_ref[...] + lb_ref[...] + bof_ref[...] + bonus
    else:
        bo_ref[...] = bonus


def _rwkv_scan(kops, v, kpars, VH, reverse, yf=None, bof=None, vpars=None):
    S, N, NCp = kops[0].shape
    NV = N // VH
    Tc = _pick(S, (32, 16, 8))
    nc = S // Tc
    tmap = (lambda g, c: (nc - 1 - c, 0, g)) if reverse else (lambda g, c: (c, 0, g))
    kspec = pl.BlockSpec((Tc, N, LANE), tmap)
    vspec = pl.BlockSpec((Tc, NV, LANE), tmap)
    kpar = pl.BlockSpec((N, LANE), lambda g, c: (0, g))
    vpar = pl.BlockSpec((NV, LANE), lambda g, c: (0, g))
    in_specs = [kspec] * 4 + [vspec] + [kpar] * 5
    args = list(kops) + [v] + list(kpars)
    vshape = jax.ShapeDtypeStruct((S, NV, NCp), F32)
    if reverse:
        in_specs += [vspec, vspec, vpar, vpar]
        args += [yf, bof] + list(vpars)
        out_specs, out_shape = vspec, vshape
    else:
        out_specs, out_shape = (vspec, vspec), (vshape, vshape)
    return pl.pallas_call(
        functools.partial(_rwkv_scan_body, Tc=Tc, N=N, VH=VH, reverse=reverse),
        grid=(NCp // LANE, nc),
        in_specs=in_specs,
        out_specs=out_specs,
        out_shape=out_shape,
        scratch_shapes=[pltpu.VMEM((N, NV, LANE), F32), pltpu.VMEM((4, Tc, N, LANE), F32),
                        pltpu.VMEM((Tc, 1, LANE), F32)],
        compiler_params=_cparams(("parallel", "arbitrary")),
        name="rwkv7_scan_bwd" if reverse else "rwkv7_scan_fwd",
    )(*args)


def _mixer_rwkv7(h2, B, S, p, x2, gt):
    T, D = h2.shape
    H, N = p["r_k"].shape
    h3 = h2.reshape(B, S, D)
    zero = jnp.zeros_like(h3[:, :1])
    prev = jnp.concatenate([zero, h3[:, :-1]], axis=1)
    nxt = jnp.concatenate([h3[:, 1:], zero], axis=1)
    xx = (0.5 * (prev + nxt) - h3).reshape(T, D)
    mu = p["mu"]
    mm = lambda n, w, **kw: _matmul(h2, w, mix=(xx, mu[n]), S=S, **kw)
    r = mm(0, p["w_r"])
    k = mm(2, p["w_k"])
    v = mm(3, p["w_v"])
    tw = mm(1, p["w1"])
    ta = mm(4, p["a1"])
    tg = mm(5, p["g1"])
    g = _matmul(tg, p["g2"], act="sigmoid", S=S)
    lw = [_matmul(tw, p["w2"][d], act="tanh", S=S, x_col_block=d) for d in range(2)]
    la = [_matmul(ta, p["a2"][d], S=S, x_col_block=d) for d in range(2)]

    NC = B * H
    VH = 2 if NC * 2 == LANE else 1
    NV = N // VH
    pad = (-VH * NC) % LANE

    def lanes(t):
        return jnp.pad(t, [(0, 0)] * (t.ndim - 1) + [(0, pad)]) if pad else t

    def kidx(t):
        return lanes(jnp.concatenate([t] * VH, axis=-1) if VH > 1 else t)

    def vidx(t):
        lead = t.shape[:-2]
        t = t.reshape(*lead, VH, NV, NC)
        t = jnp.moveaxis(t, -3, -2).reshape(*lead, NV, VH * NC)
        return lanes(t)

    def chains(t2):
        return t2.reshape(B, S, H, N).transpose(1, 3, 0, 2).reshape(S, N, NC)

    def par(t):
        return jnp.tile(t.reshape(H, N).T, (1, B))

    rc, kc, vc = kidx(chains(r)), kidx(chains(k)), vidx(chains(v))
    shared = (kidx(par(p["k_k"])), kidx(par(p["k_a"])), kidx(par(p["r_k"])))
    yf, bof = _rwkv_scan((rc, kc, kidx(chains(lw[0])), kidx(chains(la[0]))), vc,
                         (kidx(par(p["w0"][0])), kidx(par(p["a0"][0]))) + shared, VH, False)
    oc = _rwkv_scan((rc, kc, kidx(chains(lw[1])), kidx(chains(la[1]))), vc,
                    (kidx(par(p["w0"][1])), kidx(par(p["a0"][1]))) + shared, VH, True,
                    yf=yf, bof=bof, vpars=(vidx(par(p["lnx_g"])), vidx(par(p["lnx_b"]))))
    oc = oc[:, :, :VH * NC].reshape(S, NV, VH, NC).transpose(0, 2, 1, 3).reshape(S, N, B, H)
    out = oc.transpose(2, 0, 3, 1).reshape(T, D)
    return _matmul(out, p["w_o"], mul=g, res=x2, gate=gt, S=S)


def _ffn_body(x_ref, wg_ref, wu_ref, wd_ref, gt_ref, o_ref):
    f = pl.program_id(2)
    x = x_ref[0]
    g = jnp.dot(x, wg_ref[0], preferred_element_type=F32)
    u = jnp.dot(x, wu_ref[0], preferred_element_type=F32)
    hid = (_silu(g) * u).astype(BF16)
    part = jnp.dot(hid, wd_ref[0], preferred_element_type=F32)

    @pl.when(f == 0)
    def _():
        o_ref[0] = part

    @pl.when(f > 0)
    def _():
        o_ref[0] = o_ref[0] + part

    @pl.when(f == pl.num_programs(2) - 1)
    def _():
        o_ref[0] = o_ref[0] * gt_ref[0]


def _moe_ffn(xe, wg, wu, wd, gates):
    E, cap, D = xe.shape
    Fd = wg.shape[2]
    tm = _pick(cap, (512, 256, 128, 64, 32, 16, 8))
    tf = _pick(Fd, (512, 256, 128))
    return pl.pallas_call(
        _ffn_body,
        grid=(E, cap // tm, Fd // tf),
        in_specs=[
            pl.BlockSpec((1, tm, D), lambda e, m, f: (e, m, 0)),
            pl.BlockSpec((1, D, tf), lambda e, m, f: (e, 0, f)),
            pl.BlockSpec((1, D, tf), lambda e, m, f: (e, 0, f)),
            pl.BlockSpec((1, tf, D), lambda e, m, f: (e, f, 0)),
            pl.BlockSpec((1, tm, 1), lambda e, m, f: (e, m, 0)),
        ],
        out_specs=pl.BlockSpec((1, tm, D), lambda e, m, f: (e, m, 0)),
        out_shape=jax.ShapeDtypeStruct((E, cap, D), F32),
        compiler_params=_cparams(("parallel", "parallel", "arbitrary")),
        name="moe_ffn",
    )(xe, wg, wu, wd, gates.reshape(E, cap, 1))


COMB_TT = 256
COMB_W = 48


def _combine_body(off_ref, x_ref, gt_ref, ye_hbm, tok_hbm, o_ref, ybuf, tbuf, xbuf, xtok, acc_ref, sem, xsem,
                  *, E, W, TT, cap):
    i = pl.program_id(0)
    n = pl.num_programs(0)
    slot = i % 2

    def win_start(e, tile):
        st = jnp.minimum(off_ref[e, tile], cap - W)
        return pl.multiple_of((st // SUBLANE) * SUBLANE, SUBLANE)

    def ye_copy(e, tile, sl):
        return pltpu.make_async_copy(ye_hbm.at[e, pl.ds(win_start(e, tile), W), :],
                                     ybuf.at[sl, pl.ds(e * W, W), :], sem.at[sl, 0])

    def tok_copy(e, tile, sl):
        return pltpu.make_async_copy(tok_hbm.at[e, pl.ds(win_start(e, tile), W), :],
                                     tbuf.at[sl, pl.ds(e * W, W), :], sem.at[sl, 1])

    def fetch(tile, sl):
        for e in range(E):
            ye_copy(e, tile, sl).start()
            tok_copy(e, tile, sl).start()

    @pl.when(i == 0)
    def _():
        fetch(0, 0)

    @pl.when(i + 1 < n)
    def _():
        fetch(i + 1, 1 - slot)

    for e in range(E):
        ye_copy(e, i, slot).wait()
        tok_copy(e, i, slot).wait()

    dn_c = (((0,), (0,)), ((), ()))
    base = i * TT

    def onehot(tok, rows):
        lane = lax.broadcasted_iota(jnp.int32, (rows, LANE), 1)
        parts = [jnp.where(tok == base + j * LANE + lane, 1.0, 0.0) for j in range(TT // LANE)]
        return jnp.concatenate(parts, axis=1).astype(BF16)

    def scatter(oh, y):
        hi = y.astype(BF16)
        lo = (y - hi.astype(F32)).astype(BF16)
        return (lax.dot_general(oh, hi, dn_c, preferred_element_type=F32)
                + lax.dot_general(oh, lo, dn_c, preferred_element_type=F32))

    acc_ref[...] = scatter(onehot(tbuf[slot], E * W), ybuf[slot])

    def extra(e, carry):
        hi_s = off_ref[e, i + 1]

        def cond(s):
            return s < hi_s

        def body(s):
            st = pl.multiple_of(jnp.minimum(s, cap - W), SUBLANE)
            cy = pltpu.make_async_copy(ye_hbm.at[e, pl.ds(st, W), :], xbuf, xsem.at[0])
            ct = pltpu.make_async_copy(tok_hbm.at[e, pl.ds(st, W), :], xtok, xsem.at[1])
            cy.start()
            ct.start()
            cy.wait()
            ct.wait()
            rowi = lax.broadcasted_iota(jnp.int32, (W, LANE), 0)
            tok = jnp.where(st + rowi >= s, xtok[...], -1)
            acc_ref[...] = acc_ref[...] + scatter(onehot(tok, W), xbuf[...])
            return s + W

        lax.while_loop(cond, body, win_start(e, i) + W)
        return carry

    lax.fori_loop(0, E, extra, 0)
    o_ref[...] = x_ref[...] + gt_ref[0] * acc_ref[...]


def _combine(x2, gt, ye, tok, off, S):
    T, D = x2.shape
    E, cap, _ = ye.shape
    B = gt.shape[0]
    TT = _pick(S, (COMB_TT, LANE))
    W = min(COMB_W, cap)
    grid_spec = pltpu.PrefetchScalarGridSpec(
        num_scalar_prefetch=1,
        grid=(T // TT,),
        in_specs=[
            pl.BlockSpec((TT, D), lambda i, off: (i, 0)),
            pl.BlockSpec((1, 1, D), lambda i, off: ((i * TT) // S, 0, 0)),
            pl.BlockSpec(memory_space=pl.ANY),
            pl.BlockSpec(memory_space=pl.ANY),
        ],
        out_specs=pl.BlockSpec((TT, D), lambda i, off: (i, 0)),
        scratch_shapes=[
            pltpu.VMEM((2, E * W, D), F32),
            pltpu.VMEM((2, E * W, LANE), jnp.int32),
            pltpu.VMEM((W, D), F32),
            pltpu.VMEM((W, LANE), jnp.int32),
            pltpu.VMEM((TT, D), F32),
            pltpu.SemaphoreType.DMA((2, 2)),
            pltpu.SemaphoreType.DMA((2,)),
        ],
    )
    return pl.pallas_call(
        functools.partial(_combine_body, E=E, W=W, TT=TT, cap=cap),
        grid_spec=grid_spec,
        out_shape=jax.ShapeDtypeStruct((T, D), F32),
        compiler_params=_cparams(("arbitrary",)),
        name="moe_combine",
    )(off, x2, gt.reshape(B, 1, D), ye, tok)


def _ec_moe(x2, gt, h2, probs_t, p, S):
    T, D = h2.shape
    E = probs_t.shape[0]
    cap = CAPACITY_FACTOR * T // E
    gates, idx = lax.top_k(probs_t, cap)
    idx, gates = lax.sort_key_val(idx, gates, dimension=1)
    xe = h2[idx]
    ye = _moe_ffn(xe, p["w_gate"], p["w_up"], p["w_down"], gates)
    TT = _pick(S, (COMB_TT, LANE))
    starts = jnp.arange(T // TT + 1, dtype=jnp.int32) * TT
    off = jax.vmap(lambda row: jnp.searchsorted(row, starts, side="left"))(idx).astype(jnp.int32)
    tok = jnp.broadcast_to(idx[:, :, None], (E, cap, LANE))
    return _combine(x2, gt, ye, tok, off, S)


def _trunk(x, mod, P):
    B, S, D = x.shape
    T = B * S
    x2 = x.reshape(T, D)
    L = mod.shape[0]
    for layer in range(L):
        sh1, sc1, gt1, sh2, sc2, gt2 = jnp.split(mod[layer], 6, axis=-1)
        kind, j = layer % 3, layer // 3
        mix_dtype = F32 if kind == 2 else BF16
        h2 = _norm_mod(x2, P["norm_g"][layer, 0], sc1, sh1, S, mix_dtype)
        if kind == 0:
            x2 = _mixer_rglru(h2, B, S, P["lru"][j], x2, gt1)
        elif kind == 1:
            x2 = _mixer_hgrn2(h2, B, S, P["hg"][j], P["lbs"][layer:layer + 1], x2, gt1)
        else:
            x2 = _mixer_rwkv7(h2, B, S, P["r7"][j], x2, gt1)
        h2, probs_t = _norm_mod(x2, P["norm_g"][layer, 1], sc2, sh2, S, BF16,
                                router_wt=P["moe"][layer]["router_t"])
        x2 = _ec_moe(x2, gt2, h2, probs_t, P["moe"][layer], S)
    zeros = jnp.zeros((B, D), F32)
    y = _norm_mod(x2, P["final_g"], zeros, zeros, S, F32)
    return y.reshape(B, S, D)


def _pad_cols(w, width):
    return jnp.pad(w, ((0, 0), (0, width - w.shape[1])))


def _pad_rows(w, height):
    return jnp.pad(w, ((0, height - w.shape[0]), (0, 0)))


def kernel(x_prompt, x_sample, c_prompt, c_sample, ada_w, ada_b, norm_g, final_g, lru_w_in, lru_conv_w, lru_conv_b, lru_gate_w, lru_gate_b, lru_lam, lru_w_out, hg_w_in, hg_lb, hg_gn_g, hg_w_out, r7_mu, r7_w_rkv, r7_w0, r7_w1, r7_w2, r7_a0, r7_a1, r7_a2, r7_g1, r7_g2, r7_k_k, r7_k_a, r7_r_k, r7_lnx_g, r7_lnx_b, r7_w_o, moe_router, moe_w_gate, moe_w_up, moe_w_down):
    L = ada_w.shape[0]
    bf = lambda t: t.astype(BF16)

    lower = jnp.cumsum(jax.nn.softmax(hg_lb.astype(F32), axis=0), axis=0)
    P = {"norm_g": norm_g, "final_g": final_g, "lbs": lower - lower[:1]}
    P["lru"] = [dict(w_in=bf(lru_w_in[j]), conv_w=lru_conv_w[j], conv_b=lru_conv_b[j],
                     gw=bf(lru_gate_w[j]), gb=lru_gate_b[j], sp=jax.nn.softplus(-lru_lam[j]),
                     w_out=bf(lru_w_out[j])) for j in range(lru_w_in.shape[0])]
    P["hg"] = [dict(w_in=bf(hg_w_in[j]), gn_g=hg_gn_g[j].reshape(1, -1), w_out=bf(hg_w_out[j]))
               for j in range(hg_w_in.shape[0])]
    P["r7"] = []
    for j in range(r7_mu.shape[0]):
        R = r7_w1.shape[-1]
        Rp = -(-R // LANE) * LANE
        P["r7"].append(dict(
            mu=r7_mu[j], w_r=bf(r7_w_rkv[j, 0]), w_k=bf(r7_w_rkv[j, 1]), w_v=bf(r7_w_rkv[j, 2]),
            w0=r7_w0[j], a0=r7_a0[j],
            w1=bf(jnp.concatenate([_pad_cols(r7_w1[j, d], Rp) for d in range(2)], axis=1)),
            a1=bf(jnp.concatenate([_pad_cols(r7_a1[j, d], Rp) for d in range(2)], axis=1)),
            w2=bf(jnp.stack([_pad_rows(r7_w2[j, d], Rp) for d in range(2)])),
            a2=bf(jnp.stack([_pad_rows(r7_a2[j, d], Rp) for d in range(2)])),
            g1=bf(r7_g1[j]), g2=bf(r7_g2[j]), k_k=r7_k_k[j], k_a=r7_k_a[j], r_k=r7_r_k[j],
            lnx_g=r7_lnx_g[j], lnx_b=r7_lnx_b[j], w_o=bf(r7_w_o[j])))
    P["moe"] = [dict(router_t=moe_router[l].T, w_gate=bf(moe_w_gate[l]), w_up=bf(moe_w_up[l]),
                     w_down=bf(moe_w_down[l])) for l in range(L)]

    Bp, Bs = c_prompt.shape[0], c_sample.shape[0]
    c_all = jnp.concatenate([c_prompt, c_sample], axis=0)
    pad = (-c_all.shape[0]) % SUBLANE
    c_all = jnp.pad(c_all, ((0, pad), (0, 0)))
    mod = _ada_mod(c_all, ada_w, ada_b)
    y_prompt = _trunk(x_prompt, mod[:, :Bp], P)
    y_sample = _trunk(x_sample, mod[:, Bp:Bp + Bs], P)
    return (y_prompt, y_sample)
```

```python
import functools

import jax
import jax.numpy as jnp
from jax import lax
from jax.experimental import pallas as pl
from jax.experimental.pallas import tpu as pltpu

F32 = jnp.float32
BF16 = jnp.bfloat16

NORM_EPS = 1e-6
LNX_EPS = 64e-5
LRU_C = 8.0
CAPACITY_FACTOR = 2
HG_CHUNK = 32
EXP_CLAMP = 80.0

LANE = 128
SUBLANE = 8
VMEM_LIMIT = 48 * 1024 * 1024


def _cparams(sem):
    return pltpu.CompilerParams(dimension_semantics=sem, vmem_limit_bytes=VMEM_LIMIT)


def _pick(n, cands):
    for c in cands:
        if n % c == 0:
            return c
    return n


def _sigmoid(x):
    return 1.0 / (1.0 + jnp.exp(-x))


def _silu(x):
    return x * _sigmoid(x)


def _gelu_tanh(x):
    return 0.5 * x * (1.0 + jnp.tanh(0.7978845608028654 * (x + 0.044715 * (x * x * x))))


def _dot_exact_lhs(a_bf16, x):
    x1 = x.astype(BF16)
    r1 = x - x1.astype(F32)
    x2 = r1.astype(BF16)
    x3 = (r1 - x2.astype(F32)).astype(BF16)
    d = functools.partial(jnp.dot, preferred_element_type=F32)
    return d(a_bf16, x1) + d(a_bf16, x2) + d(a_bf16, x3)


def _ada_body(c_ref, w_ref, b_ref, o_ref):
    x = _silu(c_ref[...])
    acc = jnp.dot(x, w_ref[0], preferred_element_type=F32, precision=lax.Precision.HIGHEST)
    o_ref[0] = acc + b_ref[0]


def _ada_mod(c_all, ada_w, ada_b):
    L, D, N = ada_w.shape
    Mp = c_all.shape[0]
    tn = _pick(N, (1024, 512, 256, 128))
    return pl.pallas_call(
        _ada_body,
        grid=(L, N // tn),
        in_specs=[
            pl.BlockSpec((Mp, D), lambda l, j: (0, 0)),
            pl.BlockSpec((1, D, tn), lambda l, j: (l, 0, j)),
            pl.BlockSpec((1, 1, tn), lambda l, j: (l, 0, j)),
        ],
        out_specs=pl.BlockSpec((1, Mp, tn), lambda l, j: (l, 0, j)),
        out_shape=jax.ShapeDtypeStruct((L, Mp, N), F32),
        compiler_params=_cparams(("parallel", "parallel")),
        name="ada_mod",
    )(c_all, ada_w, ada_b.reshape(L, 1, N))


def _norm_body(x_ref, g_ref, sc_ref, sh_ref, *rest, with_router):
    x = x_ref[...]
    ms = jnp.mean(x * x, axis=-1, keepdims=True)
    y = x * lax.rsqrt(ms + NORM_EPS) * g_ref[...]
    h = y * (1.0 + sc_ref[0]) + sh_ref[0]
    if with_router:
        rwt_ref, o_ref, p_ref = rest
        logits = lax.dot_general(rwt_ref[...], h, (((1,), (1,)), ((), ())),
                                 preferred_element_type=F32, precision=lax.Precision.HIGHEST)
        m = jnp.max(logits, axis=0, keepdims=True)
        e = jnp.exp(logits - m)
        p_ref[...] = e / jnp.sum(e, axis=0, keepdims=True)
    else:
        (o_ref,) = rest
    o_ref[...] = h.astype(o_ref.dtype)


def _norm_mod(x2, g, sc, sh, S, out_dtype, router_wt=None):
    T, D = x2.shape
    B = sc.shape[0]
    tm = _pick(S, (512, 256, 128, 64, 32, 16, 8))
    bmap = lambda i: ((i * tm) // S, 0, 0)
    in_specs = [
        pl.BlockSpec((tm, D), lambda i: (i, 0)),
        pl.BlockSpec((1, D), lambda i: (0, 0)),
        pl.BlockSpec((1, 1, D), bmap),
        pl.BlockSpec((1, 1, D), bmap),
    ]
    args = [x2, g.reshape(1, D), sc.reshape(B, 1, D), sh.reshape(B, 1, D)]
    out_specs = pl.BlockSpec((tm, D), lambda i: (i, 0))
    out_shape = jax.ShapeDtypeStruct((T, D), out_dtype)
    if router_wt is not None:
        E = router_wt.shape[0]
        in_specs.append(pl.BlockSpec((E, D), lambda i: (0, 0)))
        args.append(router_wt)
        out_specs = (out_specs, pl.BlockSpec((E, tm), lambda i: (0, i)))
        out_shape = (out_shape, jax.ShapeDtypeStruct((E, T), F32))
    return pl.pallas_call(
        functools.partial(_norm_body, with_router=router_wt is not None),
        grid=(T // tm,),
        in_specs=in_specs,
        out_specs=out_specs,
        out_shape=out_shape,
        compiler_params=_cparams(("parallel",)),
        name="norm_mod",
    )(*args)


def _mm_body(*refs, act, has_mix, has_mul, has_res):
    it = iter(refs)
    x_ref = next(it)
    x = x_ref[...]
    if has_mul:
        x = x * next(it)[...]
    if has_mix:
        xx_ref = next(it)
        mu_ref = next(it)
        x = x.astype(F32) + xx_ref[...].astype(F32) * mu_ref[...]
    w_ref = next(it)
    if act == "tanh":
        x = jnp.tanh(x.astype(F32))
    elif act == "sigmoid":
        x = _sigmoid(x.astype(F32))
    acc = jnp.dot(x.astype(BF16), w_ref[...], preferred_element_type=F32)
    if has_res:
        res_ref = next(it)
        gt_ref = next(it)
        acc = res_ref[...] + gt_ref[0] * acc
    o_ref = next(it)
    o_ref[...] = acc.astype(o_ref.dtype)


def _matmul(x, w, *, out_dtype=F32, act=None, mix=None, mul=None, res=None, gate=None, S=None,
            x_col_block=0, tm=None, tn=None):
    M = x.shape[0]
    K, N = w.shape
    wide = mix is not None or mul is not None
    tm = tm or _pick(M if S is None else S, ((256,) if wide else (512, 256)) + (128, 64, 32, 16, 8))
    tn = tn or _pick(N, (2048, 1024, 512, 256, 128))
    cb = x_col_block
    in_specs = [pl.BlockSpec((tm, K), lambda j, i: (i, cb))]
    args = [x]
    if mul is not None:
        in_specs.append(pl.BlockSpec((tm, K), lambda j, i: (i, 0)))
        args.append(mul)
    if mix is not None:
        xx, mu = mix
        in_specs += [pl.BlockSpec((tm, K), lambda j, i: (i, 0)),
                     pl.BlockSpec((1, K), lambda j, i: (0, 0))]
        args += [xx, mu.reshape(1, K)]
    in_specs.append(pl.BlockSpec((K, tn), lambda j, i: (0, j)))
    args.append(w)
    if res is not None:
        B = gate.shape[0]
        in_specs += [pl.BlockSpec((tm, tn), lambda j, i: (i, j)),
                     pl.BlockSpec((1, 1, tn), lambda j, i: ((i * tm) // S, 0, j))]
        args += [res, gate.reshape(B, 1, N)]
    return pl.pallas_call(
        functools.partial(_mm_body, act=act, has_mix=mix is not None, has_mul=mul is not None,
                          has_res=res is not None),
        grid=(N // tn, M // tm),
        in_specs=in_specs,
        out_specs=pl.BlockSpec((tm, tn), lambda j, i: (i, j)),
        out_shape=jax.ShapeDtypeStruct((M, N), out_dtype),
        compiler_params=_cparams(("parallel", "parallel")),
        name="matmul",
    )(*args)


def _tile_scan(a, u, reverse):
    R = a.shape[0]
    r8 = lax.broadcasted_iota(jnp.int32, a.shape, 0) & (SUBLANE - 1)
    for d in (1, 2, 4):
        if reverse:
            a_sh = pltpu.roll(a, R - d, axis=0)
            u_sh = pltpu.roll(u, R - d, axis=0)
            m = r8 < SUBLANE - d
        else:
            a_sh = pltpu.roll(a, d, axis=0)
            u_sh = pltpu.roll(u, d, axis=0)
            m = r8 >= d
        u = jnp.where(m, a * u_sh + u, u)
        a = jnp.where(m, a * a_sh, a)
    return a, u


def _rglru_body(*refs, reverse, Tc, nb, bs):
    if reverse:
        (xr_ref, pv_ref, nx_ref, cw_ref, cb_ref, gw_ref, gb_ref, sp_ref,
         hf_ref, gate_ref, o_ref, carry_ref) = refs
    else:
        (xr_ref, pv_ref, nx_ref, cw_ref, cb_ref, gw_ref, gb_ref, sp_ref,
         o_ref, carry_ref) = refs
    c = pl.program_id(2)
    nc = pl.num_programs(2)
    tpos = nc - 1 - c if reverse else c

    @pl.when(c == 0)
    def _():
        carry_ref[...] = jnp.zeros_like(carry_ref)

    x = xr_ref[0]
    pv = jnp.where(tpos > 0, pv_ref[0], 0.0)
    nx = jnp.where(tpos < nc - 1, nx_ref[0], 0.0)
    xp = jnp.concatenate([pv, x, nx], axis=0)
    cw = cw_ref[...]
    xc = cb_ref[...]
    for j in range(4):
        xc = xc + xp[SUBLANE - 2 + j:SUBLANE - 2 + j + Tc] * cw[j:j + 1]
    xcb = xc.astype(BF16)
    rs, gs = [], []
    for j in range(nb):
        blk = xcb[:, j * bs:(j + 1) * bs]
        rs.append(jnp.dot(blk, gw_ref[0, j], preferred_element_type=F32))
        gs.append(jnp.dot(blk, gw_ref[1, j], preferred_element_type=F32))
    r = _sigmoid(jnp.concatenate(rs, axis=1) + gb_ref[0:1])
    ig = _sigmoid(jnp.concatenate(gs, axis=1) + gb_ref[1:2])
    log_a = (-LRU_C) * r * sp_ref[...]
    a = jnp.exp(log_a)
    u = jnp.sqrt(-jnp.tanh(log_a) * (a * a + 1.0)) * (ig * xc)
    a_loc, u_loc = _tile_scan(a, u, reverse)

    h_prev = carry_ref[0:1]
    n_tiles = Tc // SUBLANE
    order = range(n_tiles - 1, -1, -1) if reverse else range(n_tiles)
    for k in order:
        sl = slice(k * SUBLANE, (k + 1) * SUBLANE)
        ht = u_loc[sl] + a_loc[sl] * h_prev
        h_prev = ht[0:1] if reverse else ht[SUBLANE - 1:SUBLANE]
        if reverse:
            y = (hf_ref[0, sl, :] + ht) * _gelu_tanh(gate_ref[0, sl, :])
            o_ref[0, sl, :] = y.astype(o_ref.dtype)
        else:
            o_ref[0, sl, :] = ht
    carry_ref[0:1] = h_prev


def _rglru_dir(u3, conv_w, conv_b, gw, gb, sp, reverse, hf=None):
    B, S, two_dr = u3.shape
    Dr = two_dr // 2
    bs = gw.shape[-1]
    C = _pick(Dr, (512, 256, 128))
    Tc = _pick(S, (256, 128, 64, 32, 16, 8))
    nb = C // bs
    ncb = Dr // C
    nc = S // Tc
    t8 = Tc // SUBLANE
    n8 = S // SUBLANE

    def tmap(c):
        return nc - 1 - c if reverse else c

    in_specs = [
        pl.BlockSpec((1, Tc, C), lambda b, n, c: (b, tmap(c), ncb + n)),
        pl.BlockSpec((1, SUBLANE, C), lambda b, n, c: (b, jnp.maximum(tmap(c) * t8 - 1, 0), ncb + n)),
        pl.BlockSpec((1, SUBLANE, C), lambda b, n, c: (b, jnp.minimum((tmap(c) + 1) * t8, n8 - 1), ncb + n)),
        pl.BlockSpec((conv_w.shape[0], C), lambda b, n, c: (0, n)),
        pl.BlockSpec((1, C), lambda b, n, c: (0, n)),
        pl.BlockSpec((2, nb, bs, bs), lambda b, n, c: (0, n, 0, 0)),
        pl.BlockSpec((2, C), lambda b, n, c: (0, n)),
        pl.BlockSpec((1, C), lambda b, n, c: (0, n)),
    ]
    args = [u3, u3, u3, conv_w, conv_b.reshape(1, Dr), gw, gb, sp]
    if reverse:
        in_specs += [pl.BlockSpec((1, Tc, C), lambda b, n, c: (b, tmap(c), n)),
                     pl.BlockSpec((1, Tc, C), lambda b, n, c: (b, tmap(c), n))]
        args += [hf, u3]
        out_dtype = BF16
    else:
        out_dtype = F32
    return pl.pallas_call(
        functools.partial(_rglru_body, reverse=reverse, Tc=Tc, nb=nb, bs=bs),
        grid=(B, ncb, nc),
        in_specs=in_specs,
        out_specs=pl.BlockSpec((1, Tc, C), lambda b, n, c: (b, tmap(c), n)),
        out_shape=jax.ShapeDtypeStruct((B, S, Dr), out_dtype),
        scratch_shapes=[pltpu.VMEM((SUBLANE, C), F32)],
        compiler_params=_cparams(("parallel", "parallel", "arbitrary")),
        name="rglru_bwd" if reverse else "rglru_fwd",
    )(*args)


def _mixer_rglru(h2, B, S, p, x2, gt):
    u = _matmul(h2, p["w_in"], S=S)
    Dr = u.shape[1] // 2
    u3 = u.reshape(B, S, 2 * Dr)
    hf = _rglru_dir(u3, p["conv_w"], p["conv_b"], p["gw"][0], p["gb"][0], p["sp"][0:1], False)
    y = _rglru_dir(u3, p["conv_w"], p["conv_b"], p["gw"][1], p["gb"][1], p["sp"][1:2], True, hf=hf)
    return _matmul(y.reshape(B * S, Dr), p["w_out"], res=x2, gate=gt, S=S)


def _hgrn2_body(*refs, reverse, Tc, C, HB, K, V):
    if reverse:
        q_ref, z_ref, v_ref, lb_ref, of_ref, g_ref, gn_ref, o_ref, st_ref = refs
    else:
        q_ref, z_ref, v_ref, lb_ref, o_ref, st_ref = refs
    c = pl.program_id(2)

    @pl.when(c == 0)
    def _():
        st_ref[...] = jnp.zeros_like(st_ref)

    dn_t = (((1,), (1,)), ((), ()))
    dn_c = (((0,), (0,)), ((), ()))
    row = lax.broadcasted_iota(jnp.int32, (C, C), 0)
    col = lax.broadcasted_iota(jnp.int32, (C, C), 1)
    keep = (col >= row) if reverse else (col <= row)
    rowt = lax.broadcasted_iota(jnp.int32, (Tc, Tc), 0)
    colt = lax.broadcasted_iota(jnp.int32, (Tc, Tc), 1)
    causal = (colt >= rowt) if reverse else (colt <= rowt)
    tri = jnp.where(causal & ((rowt // C) == (colt // C)), 1.0, 0.0).astype(BF16)
    mid = C // 2
    nsub = Tc // C
    order = list(range(nsub - 1, -1, -1) if reverse else range(nsub))
    heads = range(HB)
    ks = [slice(h * K, (h + 1) * K) for h in heads]
    vs = [slice(h * V, (h + 1) * V) for h in heads]

    lb = lb_ref[...]
    f = lb + (1.0 - lb) * _sigmoid(z_ref[0])
    kk = 1.0 - f
    q = q_ref[0]
    vb = v_ref[0].astype(BF16)
    b = _dot_exact_lhs(tri, jnp.log(f))
    qt, kt, qe, kh, gend = {}, {}, {}, {}, {}
    for j in order:
        sl = slice(j * C, (j + 1) * C)
        bj = b[sl]
        bm = bj[mid:mid + 1]
        b_end = bj[0:1] if reverse else bj[C - 1:C]
        qt[j] = (q[sl] * jnp.exp(jnp.minimum(bj - bm, EXP_CLAMP))).astype(BF16)
        kt[j] = (kk[sl] * jnp.exp(jnp.minimum(bm - bj, EXP_CLAMP))).astype(BF16)
        qe[j] = (q[sl] * jnp.exp(bj)).astype(BF16)
        kh[j] = (kk[sl] * jnp.exp(b_end - bj)).astype(BF16)
        gend[j] = jnp.exp(b_end)
    sc = {(j, h): lax.dot_general(qt[j][:, ks[h]], kt[j][:, ks[h]], dn_t, preferred_element_type=F32)
          for j in order for h in heads}
    dst = {(j, h): lax.dot_general(vb[j * C:(j + 1) * C, vs[h]], kh[j][:, ks[h]], dn_c,
                                   preferred_element_type=F32)
           for j in order for h in heads}
    scm = {jh: jnp.where(keep, sc[jh], 0.0).astype(BF16) for jh in sc}
    ov = {(j, h): jnp.dot(scm[(j, h)], vb[j * C:(j + 1) * C, vs[h]], preferred_element_type=F32)
          for j in order for h in heads}

    st = [st_ref[h] for h in heads]
    for j in order:
        sl = slice(j * C, (j + 1) * C)
        outs = []
        for h in heads:
            oi = lax.dot_general(qe[j][:, ks[h]], st[h].astype(BF16), dn_t, preferred_element_type=F32)
            st[h] = st[h] * gend[j][:, ks[h]] + dst[(j, h)]
            o = oi + ov[(j, h)]
            if reverse:
                ot = of_ref[0, sl, vs[h]] + o
                ms = jnp.mean(ot * ot, axis=-1, keepdims=True)
                o = ot * lax.rsqrt(ms + NORM_EPS) * gn_ref[...]
            outs.append(o)
        oall = outs[0] if HB == 1 else jnp.concatenate(outs, axis=1)
        if reverse:
            o_ref[0, sl, :] = (oall * _silu(g_ref[0, sl, :])).astype(o_ref.dtype)
        else:
            o_ref[0, sl, :] = oall
    for h in heads:
        st_ref[h] = st[h]


def _hgrn2_dir(u3, lb, gn_g, H, K, V, reverse, of=None):
    B, S, _ = u3.shape
    Tc = _pick(S, (256, 128, 64, 32))
    C = min(HG_CHUNK, Tc)
    HB = _pick(H, (2, 1))
    nc = S // Tc
    nh = H // HB
    zoff = (2 if reverse else 1) * nh
    voff = (3 * H * K) // (HB * V)

    def tmap(c):
        return nc - 1 - c if reverse else c

    in_specs = [
        pl.BlockSpec((1, Tc, HB * K), lambda b, h, c: (b, tmap(c), h)),
        pl.BlockSpec((1, Tc, HB * K), lambda b, h, c: (b, tmap(c), zoff + h)),
        pl.BlockSpec((1, Tc, HB * V), lambda b, h, c: (b, tmap(c), voff + h)),
        pl.BlockSpec((1, HB * K), lambda b, h, c: (0, h)),
    ]
    args = [u3, u3, u3, lb]
    if reverse:
        in_specs += [pl.BlockSpec((1, Tc, HB * V), lambda b, h, c: (b, tmap(c), h)),
                     pl.BlockSpec((1, Tc, HB * V), lambda b, h, c: (b, tmap(c), voff + nh + h)),
                     pl.BlockSpec((1, V), lambda b, h, c: (0, 0))]
        args += [of, u3, gn_g]
        out_dtype = BF16
    else:
        out_dtype = F32
    return pl.pallas_call(
        functools.partial(_hgrn2_body, reverse=reverse, Tc=Tc, C=C, HB=HB, K=K, V=V),
        grid=(B, nh, nc),
        in_specs=in_specs,
        out_specs=pl.BlockSpec((1, Tc, HB * V), lambda b, h, c: (b, tmap(c), h)),
        out_shape=jax.ShapeDtypeStruct((B, S, H * V), out_dtype),
        scratch_shapes=[pltpu.VMEM((HB, V, K), F32)],
        compiler_params=_cparams(("parallel", "parallel", "arbitrary")),
        name="hgrn2_bwd" if reverse else "hgrn2_fwd",
    )(*args)


def _mixer_hgrn2(h2, B, S, p, lb, x2, gt):
    D = h2.shape[1]
    V = p["gn_g"].shape[-1]
    H = D // V
    K = lb.shape[-1] // H
    u = _matmul(h2, p["w_in"], S=S)
    u3 = u.reshape(B, S, u.shape[1])
    of = _hgrn2_dir(u3, lb, p["gn_g"], H, K, V, False)
    y = _hgrn2_dir(u3, lb, p["gn_g"], H, K, V, True, of=of)
    return _matmul(y.reshape(B * S, D), p["w_out"], res=x2, gate=gt, S=S)


def _rwkv_scan_body(*refs, Tc, N, VH, reverse):
    if reverse:
        (r_ref, k_ref, lw_ref, la_ref, v_ref, w0_ref, a0_ref, kkp_ref, kap_ref, rkp_ref,
         yf_ref, bof_ref, lg_ref, lb_ref, o_ref, st_ref, op_ref, bon_ref) = refs
    else:
        (r_ref, k_ref, lw_ref, la_ref, v_ref, w0_ref, a0_ref, kkp_ref, kap_ref, rkp_ref,
         y_ref, bo_ref, st_ref, op_ref, bon_ref) = refs
    yo_ref = o_ref if reverse else y_ref
    c = pl.program_id(1)

    @pl.when(c == 0)
    def _():
        st_ref[...] = jnp.zeros_like(st_ref)

    def vsum(x):
        s = jnp.sum(x, axis=1)
        for b in range(1, VH):
            s = s + pltpu.roll(s, b * (LANE // VH), axis=1)
        return s[:, None, :]

    x = w0_ref[...] + lw_ref[...]
    wl = -(jnp.maximum(-x, 0.0) + jnp.log1p(jnp.exp(-jnp.abs(x)))) - 0.5
    op_ref[0] = jnp.exp(-jnp.exp(wl))
    a = _sigmoid(a0_ref[...] + la_ref[...])
    kt = k_ref[...]
    kd = kt * (1.0 + (a - 1.0) * kap_ref[...])
    op_ref[3] = kd
    kx = kt * kkp_ref[...]
    nrm = jnp.sqrt(jnp.sum(kx * kx, axis=1, keepdims=True))
    kap = kx / jnp.maximum(nrm, 1e-12)
    op_ref[1] = -kap
    op_ref[2] = kap * a
    bon_ref[...] = jnp.sum(r_ref[...] * kd * rkp_ref[...], axis=1, keepdims=True)

    def step(i, carry):
        t = Tc - 1 - i if reverse else i
        vv = v_ref[t]
        sas = [jnp.zeros_like(vv), jnp.zeros_like(vv)]
        for k in range(N):
            sas[k % 2] = sas[k % 2] + st_ref[k] * op_ref[1, t, k:k + 1, :]
        sa = sas[0] + sas[1]
        ys = [jnp.zeros_like(vv), jnp.zeros_like(vv)]
        for k in range(N):
            s_new = (st_ref[k] * op_ref[0, t, k:k + 1, :] + sa * op_ref[2, t, k:k + 1, :]
                     + vv * op_ref[3, t, k:k + 1, :])
            st_ref[k] = s_new
            ys[k % 2] = ys[k % 2] + s_new * r_ref[t, k:k + 1, :]
        yo_ref[t] = ys[0] + ys[1]
        return carry

    lax.fori_loop(0, Tc, step, 0)

    bonus = bon_ref[...] * v_ref[...]
    if reverse:
        yt = yf_ref[...] + o_ref[...]
        yc = yt - vsum(yt) * (1.0 / N)
        var = vsum(yc * yc) * (1.0 / N)
        o_ref[...] = yc * lax.rsqrt(var + LNX_EPS) * lg_ref[...] + lb_ref[...] + bof_ref[...] + bonus
    else:
        bo_ref[...] = bonus


def _rwkv_scan(kops, v, kpars, VH, reverse, yf=None, bof=None, vpars=None):
    S, N, NCp = kops[0].shape
    NV = N // VH
    Tc = _pick(S, (32, 16, 8))
    nc = S // Tc
    tmap = (lambda g, c: (nc - 1 - c, 0, g)) if reverse else (lambda g, c: (c, 0, g))
    kspec = pl.BlockSpec((Tc, N, LANE), tmap)
    vspec = pl.BlockSpec((Tc, NV, LANE), tmap)
    kpar = pl.BlockSpec((N, LANE), lambda g, c: (0, g))
    vpar = pl.BlockSpec((NV, LANE), lambda g, c: (0, g))
    in_specs = [kspec] * 4 + [vspec] + [kpar] * 5
    args = list(kops) + [v] + list(kpars)
    vshape = jax.ShapeDtypeStruct((S, NV, NCp), F32)
    if reverse:
        in_specs += [vspec, vspec, vpar, vpar]
        args += [yf, bof] + list(vpars)
        out_specs, out_shape = vspec, vshape
    else:
        out_specs, out_shape = (vspec, vspec), (vshape, vshape)
    return pl.pallas_call(
        functools.partial(_rwkv_scan_body, Tc=Tc, N=N, VH=VH, reverse=reverse),
        grid=(NCp // LANE, nc),
        in_specs=in_specs,
        out_specs=out_specs,
        out_shape=out_shape,
        scratch_shapes=[pltpu.VMEM((N, NV, LANE), F32), pltpu.VMEM((4, Tc, N, LANE), F32),
                        pltpu.VMEM((Tc, 1, LANE), F32)],
        compiler_params=_cparams(("parallel", "arbitrary")),
        name="rwkv7_scan_bwd" if reverse else "rwkv7_scan_fwd",
    )(*args)


def _mixer_rwkv7(h2, B, S, p, x2, gt):
    T, D = h2.shape
    H, N = p["r_k"].shape
    h3 = h2.reshape(B, S, D)
    zero = jnp.zeros_like(h3[:, :1])
    prev = jnp.concatenate([zero, h3[:, :-1]], axis=1)
    nxt = jnp.concatenate([h3[:, 1:], zero], axis=1)
    xx = (0.5 * (prev + nxt) - h3).reshape(T, D)
    mu = p["mu"]
    mm = lambda n, w, **kw: _matmul(h2, w, mix=(xx, mu[n]), S=S, **kw)
    r = mm(0, p["w_r"])
    k = mm(2, p["w_k"])
    v = mm(3, p["w_v"])
    tw = mm(1, p["w1"])
    ta = mm(4, p["a1"])
    tg = mm(5, p["g1"])
    g = _matmul(tg, p["g2"], act="sigmoid", S=S)
    lw = [_matmul(tw, p["w2"][d], act="tanh", S=S, x_col_block=d) for d in range(2)]
    la = [_matmul(ta, p["a2"][d], S=S, x_col_block=d) for d in range(2)]

    NC = B * H
    VH = 2 if NC * 2 == LANE else 1
    NV = N // VH
    pad = (-VH * NC) % LANE

    def lanes(t):
        return jnp.pad(t, [(0, 0)] * (t.ndim - 1) + [(0, pad)]) if pad else t

    def kidx(t):
        return lanes(jnp.concatenate([t] * VH, axis=-1) if VH > 1 else t)

    def vidx(t):
        lead = t.shape[:-2]
        t = t.reshape(*lead, VH, NV, NC)
        t = jnp.moveaxis(t, -3, -2).reshape(*lead, NV, VH * NC)
        return lanes(t)

    def chains(t2):
        return t2.reshape(B, S, H, N).transpose(1, 3, 0, 2).reshape(S, N, NC)

    def par(t):
        return jnp.tile(t.reshape(H, N).T, (1, B))

    rc, kc, vc = kidx(chains(r)), kidx(chains(k)), vidx(chains(v))
    shared = (kidx(par(p["k_k"])), kidx(par(p["k_a"])), kidx(par(p["r_k"])))
    yf, bof = _rwkv_scan((rc, kc, kidx(chains(lw[0])), kidx(chains(la[0]))), vc,
                         (kidx(par(p["w0"][0])), kidx(par(p["a0"][0]))) + shared, VH, False)
    oc = _rwkv_scan((rc, kc, kidx(chains(lw[1])), kidx(chains(la[1]))), vc,
                    (kidx(par(p["w0"][1])), kidx(par(p["a0"][1]))) + shared, VH, True,
                    yf=yf, bof=bof, vpars=(vidx(par(p["lnx_g"])), vidx(par(p["lnx_b"]))))
    oc = oc[:, :, :VH * NC].reshape(S, NV, VH, NC).transpose(0, 2, 1, 3).reshape(S, N, B, H)
    out = oc.transpose(2, 0, 3, 1).reshape(T, D)
    return _matmul(out, p["w_o"], mul=g, res=x2, gate=gt, S=S)


def _ffn_body(x_ref, wg_ref, wu_ref, wd_ref, gt_ref, o_ref, acc_ref):
    f = pl.program_id(2)
    x = x_ref[0]
    g = jnp.dot(x, wg_ref[0], preferred_element_type=F32)
    u = jnp.dot(x, wu_ref[0], preferred_element_type=F32)
    hid = (_silu(g) * u).astype(BF16)
    part = jnp.dot(hid, wd_ref[0], preferred_element_type=F32)

    @pl.when(f == 0)
    def _():
        acc_ref[...] = part

    @pl.when(f > 0)
    def _():
        acc_ref[...] = acc_ref[...] + part

    @pl.when(f == pl.num_programs(2) - 1)
    def _():
        o_ref[0] = (acc_ref[...] * gt_ref[0]).astype(o_ref.dtype)


FFN_VMEM_LIMIT = 56 * 1024 * 1024


def _moe_ffn(xe, wg, wu, wd, gates):
    E, cap, D = xe.shape
    Fd = wg.shape[2]
    tm = _pick(cap, (1024, 512, 256, 128, 64, 32, 16, 8))
    tf = _pick(Fd, (512, 256, 128))
    return pl.pallas_call(
        _ffn_body,
        grid=(E, cap // tm, Fd // tf),
        in_specs=[
            pl.BlockSpec((1, tm, D), lambda e, m, f: (e, m, 0)),
            pl.BlockSpec((1, D, tf), lambda e, m, f: (e, 0, f)),
            pl.BlockSpec((1, D, tf), lambda e, m, f: (e, 0, f)),
            pl.BlockSpec((1, tf, D), lambda e, m, f: (e, f, 0)),
            pl.BlockSpec((1, tm, 1), lambda e, m, f: (e, m, 0)),
        ],
        out_specs=pl.BlockSpec((1, tm, D), lambda e, m, f: (e, m, 0)),
        out_shape=jax.ShapeDtypeStruct((E, cap, D), BF16),
        scratch_shapes=[pltpu.VMEM((tm, D), F32)],
        compiler_params=pltpu.CompilerParams(dimension_semantics=("parallel", "parallel", "arbitrary"),
                                             vmem_limit_bytes=FFN_VMEM_LIMIT),
        name="moe_ffn",
    )(xe, wg, wu, wd, gates.reshape(E, cap, 1))


COMB_TT = 256
COMB_W = 64


def _combine_body(off_ref, x_ref, gt_ref, ye_hbm, tok_hbm, o_ref, ybuf, tbuf, xbuf, xtok, acc_ref, sem, xsem,
                  *, E, W, TT, cap):
    i = pl.program_id(0)
    n = pl.num_programs(0)
    slot = i % 2

    ROWS = 2 * SUBLANE

    def win_start(e, tile):
        st = jnp.minimum(off_ref[e, tile], cap - W)
        return pl.multiple_of((st // ROWS) * ROWS, ROWS)

    def ye_copy(e, tile, sl):
        return pltpu.make_async_copy(ye_hbm.at[e, pl.ds(win_start(e, tile), W), :],
                                     ybuf.at[sl, pl.ds(e * W, W), :], sem.at[sl, 0])

    def tok_copy(e, tile, sl):
        return pltpu.make_async_copy(tok_hbm.at[e, pl.ds(win_start(e, tile), W), :],
                                     tbuf.at[sl, pl.ds(e * W, W), :], sem.at[sl, 1])

    def fetch(tile, sl):
        for e in range(E):
            ye_copy(e, tile, sl).start()
            tok_copy(e, tile, sl).start()

    @pl.when(i == 0)
    def _():
        fetch(0, 0)

    @pl.when(i + 1 < n)
    def _():
        fetch(i + 1, 1 - slot)

    for e in range(E):
        ye_copy(e, i, slot).wait()
        tok_copy(e, i, slot).wait()

    dn_c = (((0,), (0,)), ((), ()))
    base = i * TT

    def onehot(tok, rows):
        lane = lax.broadcasted_iota(jnp.int32, (rows, LANE), 1)
        parts = [jnp.where(tok == base + j * LANE + lane, 1.0, 0.0) for j in range(TT // LANE)]
        return jnp.concatenate(parts, axis=1).astype(BF16)

    def scatter(oh, y):
        return lax.dot_general(oh, y, dn_c, preferred_element_type=F32)

    acc_ref[...] = scatter(onehot(tbuf[slot], E * W), ybuf[slot])

    def extra(e, carry):
        hi_s = off_ref[e, i + 1]

        def cond(s):
            return s < hi_s

        def body(s):
            st = pl.multiple_of(jnp.minimum(s, cap - W), ROWS)
            cy = pltpu.make_async_copy(ye_hbm.at[e, pl.ds(st, W), :], xbuf, xsem.at[0])
            ct = pltpu.make_async_copy(tok_hbm.at[e, pl.ds(st, W), :], xtok, xsem.at[1])
            cy.start()
            ct.start()
            cy.wait()
            ct.wait()
            rowi = lax.broadcasted_iota(jnp.int32, (W, LANE), 0)
            tok = jnp.where(st + rowi >= s, xtok[...], -1)
            acc_ref[...] = acc_ref[...] + scatter(onehot(tok, W), xbuf[...])
            return s + W

        lax.while_loop(cond, body, win_start(e, i) + W)
        return carry

    lax.fori_loop(0, E, extra, 0)
    o_ref[...] = x_ref[...] + gt_ref[0] * acc_ref[...]


def _combine(x2, gt, ye, tok, off, S):
    T, D = x2.shape
    E, cap, _ = ye.shape
    B = gt.shape[0]
    TT = _pick(S, (COMB_TT, LANE))
    W = min(COMB_W, cap)
    grid_spec = pltpu.PrefetchScalarGridSpec(
        num_scalar_prefetch=1,
        grid=(T // TT,),
        in_specs=[
            pl.BlockSpec((TT, D), lambda i, off: (i, 0)),
            pl.BlockSpec((1, 1, D), lambda i, off: ((i * TT) // S, 0, 0)),
            pl.BlockSpec(memory_space=pl.ANY),
            pl.BlockSpec(memory_space=pl.ANY),
        ],
        out_specs=pl.BlockSpec((TT, D), lambda i, off: (i, 0)),
        scratch_shapes=[
            pltpu.VMEM((2, E * W, D), BF16),
            pltpu.VMEM((2, E * W, LANE), jnp.int32),
            pltpu.VMEM((W, D), BF16),
            pltpu.VMEM((W, LANE), jnp.int32),
            pltpu.VMEM((TT, D), F32),
            pltpu.SemaphoreType.DMA((2, 2)),
            pltpu.SemaphoreType.DMA((2,)),
        ],
    )
    return pl.pallas_call(
        functools.partial(_combine_body, E=E, W=W, TT=TT, cap=cap),
        grid_spec=grid_spec,
        out_shape=jax.ShapeDtypeStruct((T, D), F32),
        compiler_params=_cparams(("arbitrary",)),
        name="moe_combine",
    )(off, x2, gt.reshape(B, 1, D), ye, tok)


def _ec_moe(x2, gt, h2, probs_t, p, S):
    T, D = h2.shape
    E = probs_t.shape[0]
    cap = CAPACITY_FACTOR * T // E
    gates, idx = lax.top_k(probs_t, cap)
    idx, gates = lax.sort_key_val(idx, gates, dimension=1)
    xe = h2[idx]
    ye = _moe_ffn(xe, p["w_gate"], p["w_up"], p["w_down"], gates)
    TT = _pick(S, (COMB_TT, LANE))
    starts = jnp.arange(T // TT + 1, dtype=jnp.int32) * TT
    off = jax.vmap(lambda row: jnp.searchsorted(row, starts, side="left"))(idx).astype(jnp.int32)
    tok = jnp.broadcast_to(idx[:, :, None], (E, cap, LANE))
    return _combine(x2, gt, ye, tok, off, S)


def _trunk(x, mod, P):
    B, S, D = x.shape
    T = B * S
    x2 = x.reshape(T, D)
    L = mod.shape[0]
    for layer in range(L):
        sh1, sc1, gt1, sh2, sc2, gt2 = jnp.split(mod[layer], 6, axis=-1)
        kind, j = layer % 3, layer // 3
        mix_dtype = F32 if kind == 2 else BF16
        h2 = _norm_mod(x2, P["norm_g"][layer, 0], sc1, sh1, S, mix_dtype)
        if kind == 0:
            x2 = _mixer_rglru(h2, B, S, P["lru"][j], x2, gt1)
        elif kind == 1:
            x2 = _mixer_hgrn2(h2, B, S, P["hg"][j], P["lbs"][layer:layer + 1], x2, gt1)
        else:
            x2 = _mixer_rwkv7(h2, B, S, P["r7"][j], x2, gt1)
        h2, probs_t = _norm_mod(x2, P["norm_g"][layer, 1], sc2, sh2, S, BF16,
                                router_wt=P["moe"][layer]["router_t"])
        x2 = _ec_moe(x2, gt2, h2, probs_t, P["moe"][layer], S)
    zeros = jnp.zeros((B, D), F32)
    y = _norm_mod(x2, P["final_g"], zeros, zeros, S, F32)
    return y.reshape(B, S, D)


def _pad_cols(w, width):
    return jnp.pad(w, ((0, 0), (0, width - w.shape[1])))


def _pad_rows(w, height):
    return jnp.pad(w, ((0, height - w.shape[0]), (0, 0)))


def kernel(x_prompt, x_sample, c_prompt, c_sample, ada_w, ada_b, norm_g, final_g, lru_w_in, lru_conv_w, lru_conv_b, lru_gate_w, lru_gate_b, lru_lam, lru_w_out, hg_w_in, hg_lb, hg_gn_g, hg_w_out, r7_mu, r7_w_rkv, r7_w0, r7_w1, r7_w2, r7_a0, r7_a1, r7_a2, r7_g1, r7_g2, r7_k_k, r7_k_a, r7_r_k, r7_lnx_g, r7_lnx_b, r7_w_o, moe_router, moe_w_gate, moe_w_up, moe_w_down):
    L = ada_w.shape[0]
    bf = lambda t: t.astype(BF16)

    lower = jnp.cumsum(jax.nn.softmax(hg_lb.astype(F32), axis=0), axis=0)
    P = {"norm_g": norm_g, "final_g": final_g, "lbs": lower - lower[:1]}
    P["lru"] = [dict(w_in=bf(lru_w_in[j]), conv_w=lru_conv_w[j], conv_b=lru_conv_b[j],
                     gw=bf(lru_gate_w[j]), gb=lru_gate_b[j], sp=jax.nn.softplus(-lru_lam[j]),
                     w_out=bf(lru_w_out[j])) for j in range(lru_w_in.shape[0])]
    P["hg"] = [dict(w_in=bf(hg_w_in[j]), gn_g=hg_gn_g[j].reshape(1, -1), w_out=bf(hg_w_out[j]))
               for j in range(hg_w_in.shape[0])]
    P["r7"] = []
    for j in range(r7_mu.shape[0]):
        R = r7_w1.shape[-1]
        Rp = -(-R // LANE) * LANE
        P["r7"].append(dict(
            mu=r7_mu[j], w_r=bf(r7_w_rkv[j, 0]), w_k=bf(r7_w_rkv[j, 1]), w_v=bf(r7_w_rkv[j, 2]),
            w0=r7_w0[j], a0=r7_a0[j],
            w1=bf(jnp.concatenate([_pad_cols(r7_w1[j, d], Rp) for d in range(2)], axis=1)),
            a1=bf(jnp.concatenate([_pad_cols(r7_a1[j, d], Rp) for d in range(2)], axis=1)),
            w2=bf(jnp.stack([_pad_rows(r7_w2[j, d], Rp) for d in range(2)])),
            a2=bf(jnp.stack([_pad_rows(r7_a2[j, d], Rp) for d in range(2)])),
            g1=bf(r7_g1[j]), g2=bf(r7_g2[j]), k_k=r7_k_k[j], k_a=r7_k_a[j], r_k=r7_r_k[j],
            lnx_g=r7_lnx_g[j], lnx_b=r7_lnx_b[j], w_o=bf(r7_w_o[j])))
    P["moe"] = [dict(router_t=moe_router[l].T, w_gate=bf(moe_w_gate[l]), w_up=bf(moe_w_up[l]),
                     w_down=bf(moe_w_down[l])) for l in range(L)]

    Bp, Bs = c_prompt.shape[0], c_sample.shape[0]
    c_all = jnp.concatenate([c_prompt, c_sample], axis=0)
    pad = (-c_all.shape[0]) % SUBLANE
    c_all = jnp.pad(c_all, ((0, pad), (0, 0)))
    mod = _ada_mod(c_all, ada_w, ada_b)
    y_prompt = _trunk(x_prompt, mod[:, :Bp], P)
    y_sample = _trunk(x_sample, mod[:, Bp:Bp + Bs], P)
    return (y_prompt, y_sample)
```

```python
import functools

import jax
import jax.numpy as jnp
from jax import lax
from jax.experimental import pallas as pl
from jax.experimental.pallas import tpu as pltpu

F32 = jnp.float32
BF16 = jnp.bfloat16

NORM_EPS = 1e-6
LNX_EPS = 64e-5
LRU_C = 8.0
CAPACITY_FACTOR = 2
HG_CHUNK = 32
EXP_CLAMP = 80.0

LANE = 128
SUBLANE = 8
VMEM_LIMIT = 48 * 1024 * 1024


def _cparams(sem):
    return pltpu.CompilerParams(dimension_semantics=sem, vmem_limit_bytes=VMEM_LIMIT)


def _pick(n, cands):
    for c in cands:
        if n % c == 0:
            return c
    return n


def _sigmoid(x):
    return 1.0 / (1.0 + jnp.exp(-x))


def _silu(x):
    return x * _sigmoid(x)


def _gelu_tanh(x):
    return 0.5 * x * (1.0 + jnp.tanh(0.7978845608028654 * (x + 0.044715 * (x * x * x))))


def _dot_exact_lhs(a_bf16, x):
    x1 = x.astype(BF16)
    r1 = x - x1.astype(F32)
    x2 = r1.astype(BF16)
    x3 = (r1 - x2.astype(F32)).astype(BF16)
    d = functools.partial(jnp.dot, preferred_element_type=F32)
    return d(a_bf16, x1) + d(a_bf16, x2) + d(a_bf16, x3)


def _ada_body(c_ref, w_ref, b_ref, o_ref):
    x = _silu(c_ref[...])
    acc = jnp.dot(x, w_ref[0], preferred_element_type=F32, precision=lax.Precision.HIGHEST)
    o_ref[0] = acc + b_ref[0]


def _ada_mod(c_all, ada_w, ada_b):
    L, D, N = ada_w.shape
    Mp = c_all.shape[0]
    tn = _pick(N, (1024, 512, 256, 128))
    return pl.pallas_call(
        _ada_body,
        grid=(L, N // tn),
        in_specs=[
            pl.BlockSpec((Mp, D), lambda l, j: (0, 0)),
            pl.BlockSpec((1, D, tn), lambda l, j: (l, 0, j)),
            pl.BlockSpec((1, 1, tn), lambda l, j: (l, 0, j)),
        ],
        out_specs=pl.BlockSpec((1, Mp, tn), lambda l, j: (l, 0, j)),
        out_shape=jax.ShapeDtypeStruct((L, Mp, N), F32),
        compiler_params=_cparams(("parallel", "parallel")),
        name="ada_mod",
    )(c_all, ada_w, ada_b.reshape(L, 1, N))


def _norm_body(x_ref, g_ref, sc_ref, sh_ref, *rest, with_router):
    x = x_ref[...]
    ms = jnp.mean(x * x, axis=-1, keepdims=True)
    y = x * lax.rsqrt(ms + NORM_EPS) * g_ref[...]
    h = y * (1.0 + sc_ref[0]) + sh_ref[0]
    if with_router:
        rwt_ref, o_ref, p_ref = rest
        logits = lax.dot_general(rwt_ref[...], h, (((1,), (1,)), ((), ())),
                                 preferred_element_type=F32, precision=lax.Precision.HIGHEST)
        m = jnp.max(logits, axis=0, keepdims=True)
        e = jnp.exp(logits - m)
        p_ref[...] = e / jnp.sum(e, axis=0, keepdims=True)
    else:
        (o_ref,) = rest
    o_ref[...] = h.astype(o_ref.dtype)


def _norm_mod(x2, g, sc, sh, S, out_dtype, router_wt=None):
    T, D = x2.shape
    B = sc.shape[0]
    tm = _pick(S, (512, 256, 128, 64, 32, 16, 8))
    bmap = lambda i: ((i * tm) // S, 0, 0)
    in_specs = [
        pl.BlockSpec((tm, D), lambda i: (i, 0)),
        pl.BlockSpec((1, D), lambda i: (0, 0)),
        pl.BlockSpec((1, 1, D), bmap),
        pl.BlockSpec((1, 1, D), bmap),
    ]
    args = [x2, g.reshape(1, D), sc.reshape(B, 1, D), sh.reshape(B, 1, D)]
    out_specs = pl.BlockSpec((tm, D), lambda i: (i, 0))
    out_shape = jax.ShapeDtypeStruct((T, D), out_dtype)
    if router_wt is not None:
        E = router_wt.shape[0]
        in_specs.append(pl.BlockSpec((E, D), lambda i: (0, 0)))
        args.append(router_wt)
        out_specs = (out_specs, pl.BlockSpec((E, tm), lambda i: (0, i)))
        out_shape = (out_shape, jax.ShapeDtypeStruct((E, T), F32))
    return pl.pallas_call(
        functools.partial(_norm_body, with_router=router_wt is not None),
        grid=(T // tm,),
        in_specs=in_specs,
        out_specs=out_specs,
        out_shape=out_shape,
        compiler_params=_cparams(("parallel",)),
        name="norm_mod",
    )(*args)


def _mm_body(*refs, act, has_mix, has_mul, has_res):
    it = iter(refs)
    x_ref = next(it)
    x = x_ref[...]
    if has_mul:
        x = x * next(it)[...]
    if has_mix:
        xx_ref = next(it)
        mu_ref = next(it)
        x = x.astype(F32) + xx_ref[...].astype(F32) * mu_ref[...]
    w_ref = next(it)
    if act == "tanh":
        x = jnp.tanh(x.astype(F32))
    elif act == "sigmoid":
        x = _sigmoid(x.astype(F32))
    acc = jnp.dot(x.astype(BF16), w_ref[...], preferred_element_type=F32)
    if has_res:
        res_ref = next(it)
        gt_ref = next(it)
        acc = res_ref[...] + gt_ref[0] * acc
    o_ref = next(it)
    o_ref[...] = acc.astype(o_ref.dtype)


def _matmul(x, w, *, out_dtype=F32, act=None, mix=None, mul=None, res=None, gate=None, S=None,
            x_col_block=0, tm=None, tn=None):
    M = x.shape[0]
    K, N = w.shape
    wide = mix is not None or mul is not None
    tm = tm or _pick(M if S is None else S, ((256,) if wide else (512, 256)) + (128, 64, 32, 16, 8))
    tn = tn or _pick(N, (2048, 1024, 512, 256, 128))
    cb = x_col_block
    in_specs = [pl.BlockSpec((tm, K), lambda j, i: (i, cb))]
    args = [x]
    if mul is not None:
        in_specs.append(pl.BlockSpec((tm, K), lambda j, i: (i, 0)))
        args.append(mul)
    if mix is not None:
        xx, mu = mix
        in_specs += [pl.BlockSpec((tm, K), lambda j, i: (i, 0)),
                     pl.BlockSpec((1, K), lambda j, i: (0, 0))]
        args += [xx, mu.reshape(1, K)]
    in_specs.append(pl.BlockSpec((K, tn), lambda j, i: (0, j)))
    args.append(w)
    if res is not None:
        B = gate.shape[0]
        in_specs += [pl.BlockSpec((tm, tn), lambda j, i: (i, j)),
                     pl.BlockSpec((1, 1, tn), lambda j, i: ((i * tm) // S, 0, j))]
        args += [res, gate.reshape(B, 1, N)]
    return pl.pallas_call(
        functools.partial(_mm_body, act=act, has_mix=mix is not None, has_mul=mul is not None,
                          has_res=res is not None),
        grid=(N // tn, M // tm),
        in_specs=in_specs,
        out_specs=pl.BlockSpec((tm, tn), lambda j, i: (i, j)),
        out_shape=jax.ShapeDtypeStruct((M, N), out_dtype),
        compiler_params=_cparams(("parallel", "parallel")),
        name="matmul",
    )(*args)


def _tile_scan(a, u, reverse):
    R = a.shape[0]
    r8 = lax.broadcasted_iota(jnp.int32, a.shape, 0) & (SUBLANE - 1)
    for d in (1, 2, 4):
        if reverse:
            a_sh = pltpu.roll(a, R - d, axis=0)
            u_sh = pltpu.roll(u, R - d, axis=0)
            m = r8 < SUBLANE - d
        else:
            a_sh = pltpu.roll(a, d, axis=0)
            u_sh = pltpu.roll(u, d, axis=0)
            m = r8 >= d
        u = jnp.where(m, a * u_sh + u, u)
        a = jnp.where(m, a * a_sh, a)
    return a, u


def _rglru_body(*refs, reverse, Tc, nb, bs):
    if reverse:
        (xr_ref, pv_ref, nx_ref, cw_ref, cb_ref, gw_ref, gb_ref, sp_ref,
         hf_ref, gate_ref, o_ref, carry_ref) = refs
    else:
        (xr_ref, pv_ref, nx_ref, cw_ref, cb_ref, gw_ref, gb_ref, sp_ref,
         o_ref, carry_ref) = refs
    c = pl.program_id(2)
    nc = pl.num_programs(2)
    tpos = nc - 1 - c if reverse else c

    @pl.when(c == 0)
    def _():
        carry_ref[...] = jnp.zeros_like(carry_ref)

    x = xr_ref[0]
    pv = jnp.where(tpos > 0, pv_ref[0], 0.0)
    nx = jnp.where(tpos < nc - 1, nx_ref[0], 0.0)
    xp = jnp.concatenate([pv, x, nx], axis=0)
    cw = cw_ref[...]
    xc = cb_ref[...]
    for j in range(4):
        xc = xc + xp[SUBLANE - 2 + j:SUBLANE - 2 + j + Tc] * cw[j:j + 1]
    xcb = xc.astype(BF16)
    rs, gs = [], []
    for j in range(nb):
        blk = xcb[:, j * bs:(j + 1) * bs]
        rs.append(jnp.dot(blk, gw_ref[0, j], preferred_element_type=F32))
        gs.append(jnp.dot(blk, gw_ref[1, j], preferred_element_type=F32))
    r = _sigmoid(jnp.concatenate(rs, axis=1) + gb_ref[0:1])
    ig = _sigmoid(jnp.concatenate(gs, axis=1) + gb_ref[1:2])
    log_a = (-LRU_C) * r * sp_ref[...]
    a = jnp.exp(log_a)
    u = jnp.sqrt(-jnp.tanh(log_a) * (a * a + 1.0)) * (ig * xc)
    a_loc, u_loc = _tile_scan(a, u, reverse)

    h_prev = carry_ref[0:1]
    n_tiles = Tc // SUBLANE
    order = range(n_tiles - 1, -1, -1) if reverse else range(n_tiles)
    for k in order:
        sl = slice(k * SUBLANE, (k + 1) * SUBLANE)
        ht = u_loc[sl] + a_loc[sl] * h_prev
        h_prev = ht[0:1] if reverse else ht[SUBLANE - 1:SUBLANE]
        if reverse:
            y = (hf_ref[0, sl, :] + ht) * _gelu_tanh(gate_ref[0, sl, :])
            o_ref[0, sl, :] = y.astype(o_ref.dtype)
        else:
            o_ref[0, sl, :] = ht
    carry_ref[0:1] = h_prev


def _rglru_dir(u3, conv_w, conv_b, gw, gb, sp, reverse, hf=None):
    B, S, two_dr = u3.shape
    Dr = two_dr // 2
    bs = gw.shape[-1]
    C = _pick(Dr, (512, 256, 128))
    Tc = _pick(S, (256, 128, 64, 32, 16, 8))
    nb = C // bs
    ncb = Dr // C
    nc = S // Tc
    t8 = Tc // SUBLANE
    n8 = S // SUBLANE

    def tmap(c):
        return nc - 1 - c if reverse else c

    in_specs = [
        pl.BlockSpec((1, Tc, C), lambda b, n, c: (b, tmap(c), ncb + n)),
        pl.BlockSpec((1, SUBLANE, C), lambda b, n, c: (b, jnp.maximum(tmap(c) * t8 - 1, 0), ncb + n)),
        pl.BlockSpec((1, SUBLANE, C), lambda b, n, c: (b, jnp.minimum((tmap(c) + 1) * t8, n8 - 1), ncb + n)),
        pl.BlockSpec((conv_w.shape[0], C), lambda b, n, c: (0, n)),
        pl.BlockSpec((1, C), lambda b, n, c: (0, n)),
        pl.BlockSpec((2, nb, bs, bs), lambda b, n, c: (0, n, 0, 0)),
        pl.BlockSpec((2, C), lambda b, n, c: (0, n)),
        pl.BlockSpec((1, C), lambda b, n, c: (0, n)),
    ]
    args = [u3, u3, u3, conv_w, conv_b.reshape(1, Dr), gw, gb, sp]
    if reverse:
        in_specs += [pl.BlockSpec((1, Tc, C), lambda b, n, c: (b, tmap(c), n)),
                     pl.BlockSpec((1, Tc, C), lambda b, n, c: (b, tmap(c), n))]
        args += [hf, u3]
        out_dtype = BF16
    else:
        out_dtype = F32
    return pl.pallas_call(
        functools.partial(_rglru_body, reverse=reverse, Tc=Tc, nb=nb, bs=bs),
        grid=(B, ncb, nc),
        in_specs=in_specs,
        out_specs=pl.BlockSpec((1, Tc, C), lambda b, n, c: (b, tmap(c), n)),
        out_shape=jax.ShapeDtypeStruct((B, S, Dr), out_dtype),
        scratch_shapes=[pltpu.VMEM((SUBLANE, C), F32)],
        compiler_params=_cparams(("parallel", "parallel", "arbitrary")),
        name="rglru_bwd" if reverse else "rglru_fwd",
    )(*args)


def _mixer_rglru(h2, B, S, p, x2, gt):
    u = _matmul(h2, p["w_in"], S=S)
    Dr = u.shape[1] // 2
    u3 = u.reshape(B, S, 2 * Dr)
    hf = _rglru_dir(u3, p["conv_w"], p["conv_b"], p["gw"][0], p["gb"][0], p["sp"][0:1], False)
    y = _rglru_dir(u3, p["conv_w"], p["conv_b"], p["gw"][1], p["gb"][1], p["sp"][1:2], True, hf=hf)
    return _matmul(y.reshape(B * S, Dr), p["w_out"], res=x2, gate=gt, S=S)


def _hgrn2_body(*refs, reverse, Tc, C, HB, K, V):
    if reverse:
        q_ref, z_ref, v_ref, lb_ref, of_ref, g_ref, gn_ref, o_ref, st_ref = refs
    else:
        q_ref, z_ref, v_ref, lb_ref, o_ref, st_ref = refs
    c = pl.program_id(2)

    @pl.when(c == 0)
    def _():
        st_ref[...] = jnp.zeros_like(st_ref)

    dn_t = (((1,), (1,)), ((), ()))
    dn_c = (((0,), (0,)), ((), ()))
    row = lax.broadcasted_iota(jnp.int32, (C, C), 0)
    col = lax.broadcasted_iota(jnp.int32, (C, C), 1)
    keep = (col >= row) if reverse else (col <= row)
    rowt = lax.broadcasted_iota(jnp.int32, (Tc, Tc), 0)
    colt = lax.broadcasted_iota(jnp.int32, (Tc, Tc), 1)
    causal = (colt >= rowt) if reverse else (colt <= rowt)
    tri = jnp.where(causal & ((rowt // C) == (colt // C)), 1.0, 0.0).astype(BF16)
    mid = C // 2
    nsub = Tc // C
    order = list(range(nsub - 1, -1, -1) if reverse else range(nsub))
    heads = range(HB)
    ks = [slice(h * K, (h + 1) * K) for h in heads]
    vs = [slice(h * V, (h + 1) * V) for h in heads]

    lb = lb_ref[...]
    f = lb + (1.0 - lb) * _sigmoid(z_ref[0])
    kk = 1.0 - f
    q = q_ref[0]
    vb = v_ref[0].astype(BF16)
    b = _dot_exact_lhs(tri, jnp.log(f))
    qt, kt, qe, kh, gend = {}, {}, {}, {}, {}
    for j in order:
        sl = slice(j * C, (j + 1) * C)
        bj = b[sl]
        bm = bj[mid:mid + 1]
        b_end = bj[0:1] if reverse else bj[C - 1:C]
        qt[j] = (q[sl] * jnp.exp(jnp.minimum(bj - bm, EXP_CLAMP))).astype(BF16)
        kt[j] = (kk[sl] * jnp.exp(jnp.minimum(bm - bj, EXP_CLAMP))).astype(BF16)
        qe[j] = (q[sl] * jnp.exp(bj)).astype(BF16)
        kh[j] = (kk[sl] * jnp.exp(b_end - bj)).astype(BF16)
        gend[j] = jnp.exp(b_end)
    sc = {(j, h): lax.dot_general(qt[j][:, ks[h]], kt[j][:, ks[h]], dn_t, preferred_element_type=F32)
          for j in order for h in heads}
    dst = {(j, h): lax.dot_general(vb[j * C:(j + 1) * C, vs[h]], kh[j][:, ks[h]], dn_c,
                                   preferred_element_type=F32)
           for j in order for h in heads}
    scm = {jh: jnp.where(keep, sc[jh], 0.0).astype(BF16) for jh in sc}
    ov = {(j, h): jnp.dot(scm[(j, h)], vb[j * C:(j + 1) * C, vs[h]], preferred_element_type=F32)
          for j in order for h in heads}

    st = [st_ref[h] for h in heads]
    for j in order:
        sl = slice(j * C, (j + 1) * C)
        outs = []
        for h in heads:
            oi = lax.dot_general(qe[j][:, ks[h]], st[h].astype(BF16), dn_t, preferred_element_type=F32)
            st[h] = st[h] * gend[j][:, ks[h]] + dst[(j, h)]
            o = oi + ov[(j, h)]
            if reverse:
                ot = of_ref[0, sl, vs[h]] + o
                ms = jnp.mean(ot * ot, axis=-1, keepdims=True)
                o = ot * lax.rsqrt(ms + NORM_EPS) * gn_ref[...]
            outs.append(o)
        oall = outs[0] if HB == 1 else jnp.concatenate(outs, axis=1)
        if reverse:
            o_ref[0, sl, :] = (oall * _silu(g_ref[0, sl, :])).astype(o_ref.dtype)
        else:
            o_ref[0, sl, :] = oall
    for h in heads:
        st_ref[h] = st[h]


def _hgrn2_dir(u3, lb, gn_g, H, K, V, reverse, of=None):
    B, S, _ = u3.shape
    Tc = _pick(S, (256, 128, 64, 32))
    C = min(HG_CHUNK, Tc)
    HB = _pick(H, (2, 1))
    nc = S // Tc
    nh = H // HB
    zoff = (2 if reverse else 1) * nh
    voff = (3 * H * K) // (HB * V)

    def tmap(c):
        return nc - 1 - c if reverse else c

    in_specs = [
        pl.BlockSpec((1, Tc, HB * K), lambda b, h, c: (b, tmap(c), h)),
        pl.BlockSpec((1, Tc, HB * K), lambda b, h, c: (b, tmap(c), zoff + h)),
        pl.BlockSpec((1, Tc, HB * V), lambda b, h, c: (b, tmap(c), voff + h)),
        pl.BlockSpec((1, HB * K), lambda b, h, c: (0, h)),
    ]
    args = [u3, u3, u3, lb]
    if reverse:
        in_specs += [pl.BlockSpec((1, Tc, HB * V), lambda b, h, c: (b, tmap(c), h)),
                     pl.BlockSpec((1, Tc, HB * V), lambda b, h, c: (b, tmap(c), voff + nh + h)),
                     pl.BlockSpec((1, V), lambda b, h, c: (0, 0))]
        args += [of, u3, gn_g]
        out_dtype = BF16
    else:
        out_dtype = F32
    return pl.pallas_call(
        functools.partial(_hgrn2_body, reverse=reverse, Tc=Tc, C=C, HB=HB, K=K, V=V),
        grid=(B, nh, nc),
        in_specs=in_specs,
        out_specs=pl.BlockSpec((1, Tc, HB * V), lambda b, h, c: (b, tmap(c), h)),
        out_shape=jax.ShapeDtypeStruct((B, S, H * V), out_dtype),
        scratch_shapes=[pltpu.VMEM((HB, V, K), F32)],
        compiler_params=_cparams(("parallel", "parallel", "arbitrary")),
        name="hgrn2_bwd" if reverse else "hgrn2_fwd",
    )(*args)


def _mixer_hgrn2(h2, B, S, p, lb, x2, gt):
    D = h2.shape[1]
    V = p["gn_g"].shape[-1]
    H = D // V
    K = lb.shape[-1] // H
    u = _matmul(h2, p["w_in"], S=S)
    u3 = u.reshape(B, S, u.shape[1])
    of = _hgrn2_dir(u3, lb, p["gn_g"], H, K, V, False)
    y = _hgrn2_dir(u3, lb, p["gn_g"], H, K, V, True, of=of)
    return _matmul(y.reshape(B * S, D), p["w_out"], res=x2, gate=gt, S=S)


def _rwkv_scan_body(*refs, Tc, N, VH, reverse):
    if reverse:
        (r_ref, k_ref, lw_ref, la_ref, v_ref, w0_ref, a0_ref, kkp_ref, kap_ref, rkp_ref,
         yf_ref, bof_ref, lg_ref, lb_ref, o_ref, st_ref, op_ref, bon_ref) = refs
    else:
        (r_ref, k_ref, lw_ref, la_ref, v_ref, w0_ref, a0_ref, kkp_ref, kap_ref, rkp_ref,
         y_ref, bo_ref, st_ref, op_ref, bon_ref) = refs
    yo_ref = o_ref if reverse else y_ref
    c = pl.program_id(1)

    @pl.when(c == 0)
    def _():
        st_ref[...] = jnp.zeros_like(st_ref)

    def vsum(x):
        s = jnp.sum(x, axis=1)
        for b in range(1, VH):
            s = s + pltpu.roll(s, b * (LANE // VH), axis=1)
        return s[:, None, :]

    x = w0_ref[...] + lw_ref[...]
    wl = -(jnp.maximum(-x, 0.0) + jnp.log1p(jnp.exp(-jnp.abs(x)))) - 0.5
    op_ref[0] = jnp.exp(-jnp.exp(wl))
    a = _sigmoid(a0_ref[...] + la_ref[...])
    kt = k_ref[...]
    kd = kt * (1.0 + (a - 1.0) * kap_ref[...])
    op_ref[3] = kd
    kx = kt * kkp_ref[...]
    nrm = jnp.sqrt(jnp.sum(kx * kx, axis=1, keepdims=True))
    kap = kx / jnp.maximum(nrm, 1e-12)
    op_ref[1] = -kap
    op_ref[2] = kap * a
    bon_ref[...] = jnp.sum(r_ref[...] * kd * rkp_ref[...], axis=1, keepdims=True)

    def first_t():
        return Tc - 1 if reverse else 0

    t0 = first_t()
    sa0 = st_ref[0] * op_ref[1, t0, 0:1, :]
    for k in range(1, N):
        sa0 = sa0 + st_ref[k] * op_ref[1, t0, k:k + 1, :]

    def step(i, sa):
        t = Tc - 1 - i if reverse else i
        tn = jnp.clip(t - 1 if reverse else t + 1, 0, Tc - 1)
        vv = v_ref[t]
        ys = [None, None]
        sas = [None, None]
        for k in range(N):
            s_new = (st_ref[k] * op_ref[0, t, k:k + 1, :] + sa * op_ref[2, t, k:k + 1, :]
                     + vv * op_ref[3, t, k:k + 1, :])
            st_ref[k] = s_new
            yk = s_new * r_ref[t, k:k + 1, :]
            sk = s_new * op_ref[1, tn, k:k + 1, :]
            ys[k % 2] = yk if ys[k % 2] is None else ys[k % 2] + yk
            sas[k % 2] = sk if sas[k % 2] is None else sas[k % 2] + sk
        yo_ref[t] = ys[0] + ys[1]
        return sas[0] + sas[1]

    lax.fori_loop(0, Tc, step, sa0)

    bonus = bon_ref[...] * v_ref[...]
    if reverse:
        yt = yf_ref[...] + o_ref[...]
        yc = yt - vsum(yt) * (1.0 / N)
        var = vsum(yc * yc) * (1.0 / N)
        o_ref[...] = yc * lax.rsqrt(var + LNX_EPS) * lg_ref[...] + lb_ref[...] + bof_ref[...] + bonus
    else:
        bo_ref[...] = bonus


def _rwkv_scan(kops, v, kpars, VH, reverse, yf=None, bof=None, vpars=None):
    S, N, NCp = kops[0].shape
    NV = N // VH
    Tc = _pick(S, (32, 16, 8))
    nc = S // Tc
    tmap = (lambda g, c: (nc - 1 - c, 0, g)) if reverse else (lambda g, c: (c, 0, g))
    kspec = pl.BlockSpec((Tc, N, LANE), tmap)
    vspec = pl.BlockSpec((Tc, NV, LANE), tmap)
    kpar = pl.BlockSpec((N, LANE), lambda g, c: (0, g))
    vpar = pl.BlockSpec((NV, LANE), lambda g, c: (0, g))
    in_specs = [kspec] * 4 + [vspec] + [kpar] * 5
    args = list(kops) + [v] + list(kpars)
    vshape = jax.ShapeDtypeStruct((S, NV, NCp), F32)
    if reverse:
        in_specs += [vspec, vspec, vpar, vpar]
        args += [yf, bof] + list(vpars)
        out_specs, out_shape = vspec, vshape
    else:
        out_specs, out_shape = (vspec, vspec), (vshape, vshape)
    return pl.pallas_call(
        functools.partial(_rwkv_scan_body, Tc=Tc, N=N, VH=VH, reverse=reverse),
        grid=(NCp // LANE, nc),
        in_specs=in_specs,
        out_specs=out_specs,
        out_shape=out_shape,
        scratch_shapes=[pltpu.VMEM((N, NV, LANE), F32), pltpu.VMEM((4, Tc, N, LANE), F32),
                        pltpu.VMEM((Tc, 1, LANE), F32)],
        compiler_params=_cparams(("parallel", "arbitrary")),
        name="rwkv7_scan_bwd" if reverse else "rwkv7_scan_fwd",
    )(*args)


def _mixer_rwkv7(h2, B, S, p, x2, gt):
    T, D = h2.shape
    H, N = p["r_k"].shape
    h3 = h2.reshape(B, S, D)
    zero = jnp.zeros_like(h3[:, :1])
    prev = jnp.concatenate([zero, h3[:, :-1]], axis=1)
    nxt = jnp.concatenate([h3[:, 1:], zero], axis=1)
    xx = (0.5 * (prev + nxt) - h3).reshape(T, D)
    mu = p["mu"]
    mm = lambda n, w, **kw: _matmul(h2, w, mix=(xx, mu[n]), S=S, **kw)
    r = mm(0, p["w_r"])
    k = mm(2, p["w_k"])
    v = mm(3, p["w_v"])
    tw = mm(1, p["w1"])
    ta = mm(4, p["a1"])
    tg = mm(5, p["g1"])
    g = _matmul(tg, p["g2"], act="sigmoid", S=S)
    lw = [_matmul(tw, p["w2"][d], act="tanh", S=S, x_col_block=d) for d in range(2)]
    la = [_matmul(ta, p["a2"][d], S=S, x_col_block=d) for d in range(2)]

    NC = B * H
    VH = 2 if NC * 2 == LANE else 1
    NV = N // VH
    pad = (-VH * NC) % LANE

    def lanes(t):
        return jnp.pad(t, [(0, 0)] * (t.ndim - 1) + [(0, pad)]) if pad else t

    def kidx(t):
        return lanes(jnp.concatenate([t] * VH, axis=-1) if VH > 1 else t)

    def vidx(t):
        lead = t.shape[:-2]
        t = t.reshape(*lead, VH, NV, NC)
        t = jnp.moveaxis(t, -3, -2).reshape(*lead, NV, VH * NC)
        return lanes(t)

    def chains(t2):
        return t2.reshape(B, S, H, N).transpose(1, 3, 0, 2).reshape(S, N, NC)

    def par(t):
        return jnp.tile(t.reshape(H, N).T, (1, B))

    rc, kc, vc = kidx(chains(r)), kidx(chains(k)), vidx(chains(v))
    shared = (kidx(par(p["k_k"])), kidx(par(p["k_a"])), kidx(par(p["r_k"])))
    yf, bof = _rwkv_scan((rc, kc, kidx(chains(lw[0])), kidx(chains(la[0]))), vc,
                         (kidx(par(p["w0"][0])), kidx(par(p["a0"][0]))) + shared, VH, False)
    oc = _rwkv_scan((rc, kc, kidx(chains(lw[1])), kidx(chains(la[1]))), vc,
                    (kidx(par(p["w0"][1])), kidx(par(p["a0"][1]))) + shared, VH, True,
                    yf=yf, bof=bof, vpars=(vidx(par(p["lnx_g"])), vidx(par(p["lnx_b"]))))
    oc = oc[:, :, :VH * NC].reshape(S, NV, VH, NC).transpose(0, 2, 1, 3).reshape(S, N, B, H)
    out = oc.transpose(2, 0, 3, 1).reshape(T, D)
    return _matmul(out, p["w_o"], mul=g, res=x2, gate=gt, S=S)


def _ffn_body(x_ref, wg_ref, wu_ref, wd_ref, gt_ref, o_ref, acc_ref):
    f = pl.program_id(2)
    x = x_ref[0]
    g = jnp.dot(x, wg_ref[0], preferred_element_type=F32)
    u = jnp.dot(x, wu_ref[0], preferred_element_type=F32)
    hid = (_silu(g) * u).astype(BF16)
    part = jnp.dot(hid, wd_ref[0], preferred_element_type=F32)

    @pl.when(f == 0)
    def _():
        acc_ref[...] = part

    @pl.when(f > 0)
    def _():
        acc_ref[...] = acc_ref[...] + part

    @pl.when(f == pl.num_programs(2) - 1)
    def _():
        o_ref[0] = (acc_ref[...] * gt_ref[0]).astype(o_ref.dtype)


FFN_VMEM_LIMIT = 56 * 1024 * 1024


def _moe_ffn(xe, wg, wu, wd, gates):
    E, cap, D = xe.shape
    Fd = wg.shape[2]
    tm = _pick(cap, (1024, 512, 256, 128, 64, 32, 16, 8))
    tf = _pick(Fd, (512, 256, 128))
    return pl.pallas_call(
        _ffn_body,
        grid=(E, cap // tm, Fd // tf),
        in_specs=[
            pl.BlockSpec((1, tm, D), lambda e, m, f: (e, m, 0)),
            pl.BlockSpec((1, D, tf), lambda e, m, f: (e, 0, f)),
            pl.BlockSpec((1, D, tf), lambda e, m, f: (e, 0, f)),
            pl.BlockSpec((1, tf, D), lambda e, m, f: (e, f, 0)),
            pl.BlockSpec((1, tm, 1), lambda e, m, f: (e, m, 0)),
        ],
        out_specs=pl.BlockSpec((1, tm, D), lambda e, m, f: (e, m, 0)),
        out_shape=jax.ShapeDtypeStruct((E, cap, D), BF16),
        scratch_shapes=[pltpu.VMEM((tm, D), F32)],
        compiler_params=pltpu.CompilerParams(dimension_semantics=("parallel", "parallel", "arbitrary"),
                                             vmem_limit_bytes=FFN_VMEM_LIMIT),
        name="moe_ffn",
    )(xe, wg, wu, wd, gates.reshape(E, cap, 1))


COMB_TT = 256
COMB_W = 64


def _combine_body(off_ref, x_ref, gt_ref, ye_hbm, tok_hbm, o_ref, ybuf, tbuf, xbuf, xtok, acc_ref, sem, xsem,
                  *, E, W, TT, cap):
    i = pl.program_id(0)
    n = pl.num_programs(0)
    slot = i % 2

    ROWS = 2 * SUBLANE

    def win_start(e, tile):
        st = jnp.minimum(off_ref[e, tile], cap - W)
        return pl.multiple_of((st // ROWS) * ROWS, ROWS)

    def ye_copy(e, tile, sl):
        return pltpu.make_async_copy(ye_hbm.at[e, pl.ds(win_start(e, tile), W), :],
                                     ybuf.at[sl, pl.ds(e * W, W), :], sem.at[sl, 0])

    def tok_copy(e, tile, sl):
        return pltpu.make_async_copy(tok_hbm.at[e, pl.ds(win_start(e, tile), W), :],
                                     tbuf.at[sl, pl.ds(e * W, W), :], sem.at[sl, 1])

    def fetch(tile, sl):
        for e in range(E):
            ye_copy(e, tile, sl).start()
            tok_copy(e, tile, sl).start()

    @pl.when(i == 0)
    def _():
        fetch(0, 0)

    @pl.when(i + 1 < n)
    def _():
        fetch(i + 1, 1 - slot)

    for e in range(E):
        ye_copy(e, i, slot).wait()
        tok_copy(e, i, slot).wait()

    dn_c = (((0,), (0,)), ((), ()))
    base = i * TT

    def onehot(tok, rows):
        lane = lax.broadcasted_iota(jnp.int32, (rows, LANE), 1)
        parts = [jnp.where(tok == base + j * LANE + lane, 1.0, 0.0) for j in range(TT // LANE)]
        return jnp.concatenate(parts, axis=1).astype(BF16)

    def scatter(oh, y):
        return lax.dot_general(oh, y, dn_c, preferred_element_type=F32)

    acc_ref[...] = scatter(onehot(tbuf[slot], E * W), ybuf[slot])

    def extra(e, carry):
        hi_s = off_ref[e, i + 1]

        def cond(s):
            return s < hi_s

        def body(s):
            st = pl.multiple_of(jnp.minimum(s, cap - W), ROWS)
            cy = pltpu.make_async_copy(ye_hbm.at[e, pl.ds(st, W), :], xbuf, xsem.at[0])
            ct = pltpu.make_async_copy(tok_hbm.at[e, pl.ds(st, W), :], xtok, xsem.at[1])
            cy.start()
            ct.start()
            cy.wait()
            ct.wait()
            rowi = lax.broadcasted_iota(jnp.int32, (W, LANE), 0)
            tok = jnp.where(st + rowi >= s, xtok[...], -1)
            acc_ref[...] = acc_ref[...] + scatter(onehot(tok, W), xbuf[...])
            return s + W

        lax.while_loop(cond, body, win_start(e, i) + W)
        return carry

    lax.fori_loop(0, E, extra, 0)
    o_ref[...] = x_ref[...] + gt_ref[0] * acc_ref[...]


def _combine(x2, gt, ye, tok, off, S):
    T, D = x2.shape
    E, cap, _ = ye.shape
    B = gt.shape[0]
    TT = _pick(S, (COMB_TT, LANE))
    W = min(COMB_W, cap)
    grid_spec = pltpu.PrefetchScalarGridSpec(
        num_scalar_prefetch=1,
        grid=(T // TT,),
        in_specs=[
            pl.BlockSpec((TT, D), lambda i, off: (i, 0)),
            pl.BlockSpec((1, 1, D), lambda i, off: ((i * TT) // S, 0, 0)),
            pl.BlockSpec(memory_space=pl.ANY),
            pl.BlockSpec(memory_space=pl.ANY),
        ],
        out_specs=pl.BlockSpec((TT, D), lambda i, off: (i, 0)),
        scratch_shapes=[
            pltpu.VMEM((2, E * W, D), BF16),
            pltpu.VMEM((2, E * W, LANE), jnp.int32),
            pltpu.VMEM((W, D), BF16),
            pltpu.VMEM((W, LANE), jnp.int32),
            pltpu.VMEM((TT, D), F32),
            pltpu.SemaphoreType.DMA((2, 2)),
            pltpu.SemaphoreType.DMA((2,)),
        ],
    )
    return pl.pallas_call(
        functools.partial(_combine_body, E=E, W=W, TT=TT, cap=cap),
        grid_spec=grid_spec,
        out_shape=jax.ShapeDtypeStruct((T, D), F32),
        compiler_params=_cparams(("arbitrary",)),
        name="moe_combine",
    )(off, x2, gt.reshape(B, 1, D), ye, tok)


def _ec_moe(x2, gt, h2, probs_t, p, S):
    T, D = h2.shape
    E = probs_t.shape[0]
    cap = CAPACITY_FACTOR * T // E
    gates, idx = lax.top_k(probs_t, cap)
    idx, gates = lax.sort_key_val(idx, gates, dimension=1)
    xe = h2.at[idx].get(mode="promise_in_bounds")
    ye = _moe_ffn(xe, p["w_gate"], p["w_up"], p["w_down"], gates)
    TT = _pick(S, (COMB_TT, LANE))
    starts = jnp.arange(T // TT + 1, dtype=jnp.int32) * TT
    off = jnp.sum(idx[:, None, :] < starts[None, :, None], axis=-1, dtype=jnp.int32)
    tok = jnp.broadcast_to(idx[:, :, None], (E, cap, LANE))
    return _combine(x2, gt, ye, tok, off, S)


def _trunk(x, mod, P):
    B, S, D = x.shape
    T = B * S
    x2 = x.reshape(T, D)
    L = mod.shape[0]
    for layer in range(L):
        sh1, sc1, gt1, sh2, sc2, gt2 = jnp.split(mod[layer], 6, axis=-1)
        kind, j = layer % 3, layer // 3
        mix_dtype = F32 if kind == 2 else BF16
        h2 = _norm_mod(x2, P["norm_g"][layer, 0], sc1, sh1, S, mix_dtype)
        if kind == 0:
            x2 = _mixer_rglru(h2, B, S, P["lru"][j], x2, gt1)
        elif kind == 1:
            x2 = _mixer_hgrn2(h2, B, S, P["hg"][j], P["lbs"][layer:layer + 1], x2, gt1)
        else:
            x2 = _mixer_rwkv7(h2, B, S, P["r7"][j], x2, gt1)
        h2, probs_t = _norm_mod(x2, P["norm_g"][layer, 1], sc2, sh2, S, BF16,
                                router_wt=P["moe"][layer]["router_t"])
        x2 = _ec_moe(x2, gt2, h2, probs_t, P["moe"][layer], S)
    zeros = jnp.zeros((B, D), F32)
    y = _norm_mod(x2, P["final_g"], zeros, zeros, S, F32)
    return y.reshape(B, S, D)


def _pad_cols(w, width):
    return jnp.pad(w, ((0, 0), (0, width - w.shape[1])))


def _pad_rows(w, height):
    return jnp.pad(w, ((0, height - w.shape[0]), (0, 0)))


def kernel(x_prompt, x_sample, c_prompt, c_sample, ada_w, ada_b, norm_g, final_g, lru_w_in, lru_conv_w, lru_conv_b, lru_gate_w, lru_gate_b, lru_lam, lru_w_out, hg_w_in, hg_lb, hg_gn_g, hg_w_out, r7_mu, r7_w_rkv, r7_w0, r7_w1, r7_w2, r7_a0, r7_a1, r7_a2, r7_g1, r7_g2, r7_k_k, r7_k_a, r7_r_k, r7_lnx_g, r7_lnx_b, r7_w_o, moe_router, moe_w_gate, moe_w_up, moe_w_down):
    L = ada_w.shape[0]
    bf = lambda t: t.astype(BF16)

    lower = jnp.cumsum(jax.nn.softmax(hg_lb.astype(F32), axis=0), axis=0)
    P = {"norm_g": norm_g, "final_g": final_g, "lbs": lower - lower[:1]}
    P["lru"] = [dict(w_in=bf(lru_w_in[j]), conv_w=lru_conv_w[j], conv_b=lru_conv_b[j],
                     gw=bf(lru_gate_w[j]), gb=lru_gate_b[j], sp=jax.nn.softplus(-lru_lam[j]),
                     w_out=bf(lru_w_out[j])) for j in range(lru_w_in.shape[0])]
    P["hg"] = [dict(w_in=bf(hg_w_in[j]), gn_g=hg_gn_g[j].reshape(1, -1), w_out=bf(hg_w_out[j]))
               for j in range(hg_w_in.shape[0])]
    P["r7"] = []
    for j in range(r7_mu.shape[0]):
        R = r7_w1.shape[-1]
        Rp = -(-R // LANE) * LANE
        P["r7"].append(dict(
            mu=r7_mu[j], w_r=bf(r7_w_rkv[j, 0]), w_k=bf(r7_w_rkv[j, 1]), w_v=bf(r7_w_rkv[j, 2]),
            w0=r7_w0[j], a0=r7_a0[j],
            w1=bf(jnp.concatenate([_pad_cols(r7_w1[j, d], Rp) for d in range(2)], axis=1)),
            a1=bf(jnp.concatenate([_pad_cols(r7_a1[j, d], Rp) for d in range(2)], axis=1)),
            w2=bf(jnp.stack([_pad_rows(r7_w2[j, d], Rp) for d in range(2)])),
            a2=bf(jnp.stack([_pad_rows(r7_a2[j, d], Rp) for d in range(2)])),
            g1=bf(r7_g1[j]), g2=bf(r7_g2[j]), k_k=r7_k_k[j], k_a=r7_k_a[j], r_k=r7_r_k[j],
            lnx_g=r7_lnx_g[j], lnx_b=r7_lnx_b[j], w_o=bf(r7_w_o[j])))
    P["moe"] = [dict(router_t=moe_router[l].T, w_gate=bf(moe_w_gate[l]), w_up=bf(moe_w_up[l]),
                     w_down=bf(moe_w_down[l])) for l in range(L)]

    Bp, Bs = c_prompt.shape[0], c_sample.shape[0]
    c_all = jnp.concatenate([c_prompt, c_sample], axis=0)
    pad = (-c_all.shape[0]) % SUBLANE
    c_all = jnp.pad(c_all, ((0, pad), (0, 0)))
    mod = _ada_mod(c_all, ada_w, ada_b)
    y_prompt = _trunk(x_prompt, mod[:, :Bp], P)
    y_sample = _trunk(x_sample, mod[:, Bp:Bp + Bs], P)
    return (y_prompt, y_sample)
```

```python
import functools

import jax
import jax.numpy as jnp
from jax import lax
from jax.experimental import pallas as pl
from jax.experimental.pallas import tpu as pltpu

F32 = jnp.float32
BF16 = jnp.bfloat16

NORM_EPS = 1e-6
LNX_EPS = 64e-5
LRU_C = 8.0
CAPACITY_FACTOR = 2
HG_CHUNK = 32
EXP_CLAMP = 80.0

LANE = 128
SUBLANE = 8
VMEM_LIMIT = 48 * 1024 * 1024


def _cparams(sem):
    return pltpu.CompilerParams(dimension_semantics=sem, vmem_limit_bytes=VMEM_LIMIT)


def _pick(n, cands):
    for c in cands:
        if n % c == 0:
            return c
    return n


def _sigmoid(x):
    return 1.0 / (1.0 + jnp.exp(-x))


def _silu(x):
    return x * _sigmoid(x)


def _gelu_tanh(x):
    return 0.5 * x * (1.0 + jnp.tanh(0.7978845608028654 * (x + 0.044715 * (x * x * x))))


def _dot_exact_lhs(a_bf16, x):
    x1 = x.astype(BF16)
    r1 = x - x1.astype(F32)
    x2 = r1.astype(BF16)
    x3 = (r1 - x2.astype(F32)).astype(BF16)
    d = functools.partial(jnp.dot, preferred_element_type=F32)
    return d(a_bf16, x1) + d(a_bf16, x2) + d(a_bf16, x3)


def _ada_body(c_ref, w_ref, b_ref, o_ref):
    x = _silu(c_ref[...])
    acc = jnp.dot(x, w_ref[0], preferred_element_type=F32, precision=lax.Precision.HIGHEST)
    o_ref[0] = acc + b_ref[0]


def _ada_mod(c_all, ada_w, ada_b):
    L, D, N = ada_w.shape
    Mp = c_all.shape[0]
    tn = _pick(N, (1024, 512, 256, 128))
    return pl.pallas_call(
        _ada_body,
        grid=(L, N // tn),
        in_specs=[
            pl.BlockSpec((Mp, D), lambda l, j: (0, 0)),
            pl.BlockSpec((1, D, tn), lambda l, j: (l, 0, j)),
            pl.BlockSpec((1, 1, tn), lambda l, j: (l, 0, j)),
        ],
        out_specs=pl.BlockSpec((1, Mp, tn), lambda l, j: (l, 0, j)),
        out_shape=jax.ShapeDtypeStruct((L, Mp, N), F32),
        compiler_params=_cparams(("parallel", "parallel")),
        name="ada_mod",
    )(c_all, ada_w, ada_b.reshape(L, 1, N))


def _norm_body(x_ref, g_ref, sc_ref, sh_ref, *rest, with_router):
    x = x_ref[...]
    ms = jnp.mean(x * x, axis=-1, keepdims=True)
    y = x * lax.rsqrt(ms + NORM_EPS) * g_ref[...]
    h = y * (1.0 + sc_ref[0]) + sh_ref[0]
    if with_router:
        rwt_ref, o_ref, p_ref = rest
        logits = lax.dot_general(rwt_ref[...], h, (((1,), (1,)), ((), ())),
                                 preferred_element_type=F32, precision=lax.Precision.HIGHEST)
        m = jnp.max(logits, axis=0, keepdims=True)
        e = jnp.exp(logits - m)
        p_ref[...] = e / jnp.sum(e, axis=0, keepdims=True)
    else:
        (o_ref,) = rest
    o_ref[...] = h.astype(o_ref.dtype)


def _norm_mod(x2, g, sc, sh, S, out_dtype, router_wt=None):
    T, D = x2.shape
    B = sc.shape[0]
    tm = _pick(S, (512, 256, 128, 64, 32, 16, 8))
    bmap = lambda i: ((i * tm) // S, 0, 0)
    in_specs = [
        pl.BlockSpec((tm, D), lambda i: (i, 0)),
        pl.BlockSpec((1, D), lambda i: (0, 0)),
        pl.BlockSpec((1, 1, D), bmap),
        pl.BlockSpec((1, 1, D), bmap),
    ]
    args = [x2, g.reshape(1, D), sc.reshape(B, 1, D), sh.reshape(B, 1, D)]
    out_specs = pl.BlockSpec((tm, D), lambda i: (i, 0))
    out_shape = jax.ShapeDtypeStruct((T, D), out_dtype)
    if router_wt is not None:
        E = router_wt.shape[0]
        in_specs.append(pl.BlockSpec((E, D), lambda i: (0, 0)))
        args.append(router_wt)
        out_specs = (out_specs, pl.BlockSpec((E, tm), lambda i: (0, i)))
        out_shape = (out_shape, jax.ShapeDtypeStruct((E, T), F32))
    return pl.pallas_call(
        functools.partial(_norm_body, with_router=router_wt is not None),
        grid=(T // tm,),
        in_specs=in_specs,
        out_specs=out_specs,
        out_shape=out_shape,
        compiler_params=_cparams(("parallel",)),
        name="norm_mod",
    )(*args)


def _mm_body(*refs, act, has_mix, has_mul, has_res):
    it = iter(refs)
    x_ref = next(it)
    x = x_ref[...]
    if has_mul:
        x = x * next(it)[...]
    if has_mix:
        xx_ref = next(it)
        mu_ref = next(it)
        x = x.astype(F32) + xx_ref[...].astype(F32) * mu_ref[0]
    w_ref = next(it)
    if act == "tanh":
        x = jnp.tanh(x.astype(F32))
    elif act == "sigmoid":
        x = _sigmoid(x.astype(F32))
    acc = jnp.dot(x.astype(BF16), w_ref[...], preferred_element_type=F32)
    if has_res:
        res_ref = next(it)
        gt_ref = next(it)
        acc = res_ref[...] + gt_ref[0] * acc
    o_ref = next(it)
    o_ref[...] = acc.astype(o_ref.dtype)


def _matmul(x, w, *, out_dtype=F32, act=None, mix=None, mul=None, res=None, gate=None, S=None,
            x_col_block=0, tm=None, tn=None, lhs_major=False):
    M = x.shape[0]
    K, N = w.shape
    wide = mix is not None or mul is not None
    tm = tm or _pick(M if S is None else S, ((256,) if wide else (512, 256)) + (128, 64, 32, 16, 8))
    tn = tn or _pick(N, (2048, 1024, 512, 256, 128))
    cb = x_col_block
    if lhs_major:
        grid = (M // tm, N // tn)
        ij = lambda f: (lambda i, j: f(i, j))
    else:
        grid = (N // tn, M // tm)
        ij = lambda f: (lambda j, i: f(i, j))
    in_specs = [pl.BlockSpec((tm, K), ij(lambda i, j: (i, cb)))]
    args = [x]
    if mul is not None:
        in_specs.append(pl.BlockSpec((tm, K), ij(lambda i, j: (i, 0))))
        args.append(mul)
    if mix is not None:
        xx, mu = mix
        mu = mu.reshape(-1, K)
        per_tile = mu.shape[0] > 1
        in_specs += [pl.BlockSpec((tm, K), ij(lambda i, j: (i, 0))),
                     pl.BlockSpec((1, 1, K), ij(lambda i, j: (j if per_tile else 0, 0, 0)))]
        args += [xx, mu.reshape(-1, 1, K)]
    in_specs.append(pl.BlockSpec((K, tn), ij(lambda i, j: (0, j))))
    args.append(w)
    if res is not None:
        B = gate.shape[0]
        in_specs += [pl.BlockSpec((tm, tn), ij(lambda i, j: (i, j))),
                     pl.BlockSpec((1, 1, tn), ij(lambda i, j: ((i * tm) // S, 0, j)))]
        args += [res, gate.reshape(B, 1, N)]
    return pl.pallas_call(
        functools.partial(_mm_body, act=act, has_mix=mix is not None, has_mul=mul is not None,
                          has_res=res is not None),
        grid=grid,
        in_specs=in_specs,
        out_specs=pl.BlockSpec((tm, tn), ij(lambda i, j: (i, j))),
        out_shape=jax.ShapeDtypeStruct((M, N), out_dtype),
        compiler_params=_cparams(("parallel", "parallel")),
        name="matmul",
    )(*args)


def _tile_scan(a, u, reverse):
    R = a.shape[0]
    r8 = lax.broadcasted_iota(jnp.int32, a.shape, 0) & (SUBLANE - 1)
    for d in (1, 2, 4):
        if reverse:
            a_sh = pltpu.roll(a, R - d, axis=0)
            u_sh = pltpu.roll(u, R - d, axis=0)
            m = r8 < SUBLANE - d
        else:
            a_sh = pltpu.roll(a, d, axis=0)
            u_sh = pltpu.roll(u, d, axis=0)
            m = r8 >= d
        u = jnp.where(m, a * u_sh + u, u)
        a = jnp.where(m, a * a_sh, a)
    return a, u


def _rglru_body(*refs, reverse, Tc, nb, bs):
    if reverse:
        (xr_ref, pv_ref, nx_ref, cw_ref, cb_ref, gw_ref, gb_ref, sp_ref,
         hf_ref, gate_ref, o_ref, carry_ref) = refs
    else:
        (xr_ref, pv_ref, nx_ref, cw_ref, cb_ref, gw_ref, gb_ref, sp_ref,
         o_ref, carry_ref) = refs
    c = pl.program_id(2)
    nc = pl.num_programs(2)
    tpos = nc - 1 - c if reverse else c

    @pl.when(c == 0)
    def _():
        carry_ref[...] = jnp.zeros_like(carry_ref)

    x = xr_ref[0]
    pv = jnp.where(tpos > 0, pv_ref[0], 0.0)
    nx = jnp.where(tpos < nc - 1, nx_ref[0], 0.0)
    xp = jnp.concatenate([pv, x, nx], axis=0)
    cw = cw_ref[...]
    xc = cb_ref[...]
    for j in range(4):
        xc = xc + xp[SUBLANE - 2 + j:SUBLANE - 2 + j + Tc] * cw[j:j + 1]
    xcb = xc.astype(BF16)
    rs, gs = [], []
    for j in range(nb):
        blk = xcb[:, j * bs:(j + 1) * bs]
        rs.append(jnp.dot(blk, gw_ref[0, j], preferred_element_type=F32))
        gs.append(jnp.dot(blk, gw_ref[1, j], preferred_element_type=F32))
    r = _sigmoid(jnp.concatenate(rs, axis=1) + gb_ref[0:1])
    ig = _sigmoid(jnp.concatenate(gs, axis=1) + gb_ref[1:2])
    log_a = (-LRU_C) * r * sp_ref[...]
    a = jnp.exp(log_a)
    u = jnp.sqrt(-jnp.tanh(log_a) * (a * a + 1.0)) * (ig * xc)
    a_loc, u_loc = _tile_scan(a, u, reverse)

    h_prev = carry_ref[0:1]
    n_tiles = Tc // SUBLANE
    order = range(n_tiles - 1, -1, -1) if reverse else range(n_tiles)
    for k in order:
        sl = slice(k * SUBLANE, (k + 1) * SUBLANE)
        ht = u_loc[sl] + a_loc[sl] * h_prev
        h_prev = ht[0:1] if reverse else ht[SUBLANE - 1:SUBLANE]
        if reverse:
            y = (hf_ref[0, sl, :] + ht) * _gelu_tanh(gate_ref[0, sl, :])
            o_ref[0, sl, :] = y.astype(o_ref.dtype)
        else:
            o_ref[0, sl, :] = ht
    carry_ref[0:1] = h_prev


def _rglru_dir(u3, conv_w, conv_b, gw, gb, sp, reverse, hf=None):
    B, S, two_dr = u3.shape
    Dr = two_dr // 2
    bs = gw.shape[-1]
    C = _pick(Dr, (512, 256, 128))
    Tc = _pick(S, (256, 128, 64, 32, 16, 8))
    nb = C // bs
    ncb = Dr // C
    nc = S // Tc
    t8 = Tc // SUBLANE
    n8 = S // SUBLANE

    def tmap(c):
        return nc - 1 - c if reverse else c

    in_specs = [
        pl.BlockSpec((1, Tc, C), lambda b, n, c: (b, tmap(c), ncb + n)),
        pl.BlockSpec((1, SUBLANE, C), lambda b, n, c: (b, jnp.maximum(tmap(c) * t8 - 1, 0), ncb + n)),
        pl.BlockSpec((1, SUBLANE, C), lambda b, n, c: (b, jnp.minimum((tmap(c) + 1) * t8, n8 - 1), ncb + n)),
        pl.BlockSpec((conv_w.shape[0], C), lambda b, n, c: (0, n)),
        pl.BlockSpec((1, C), lambda b, n, c: (0, n)),
        pl.BlockSpec((2, nb, bs, bs), lambda b, n, c: (0, n, 0, 0)),
        pl.BlockSpec((2, C), lambda b, n, c: (0, n)),
        pl.BlockSpec((1, C), lambda b, n, c: (0, n)),
    ]
    args = [u3, u3, u3, conv_w, conv_b.reshape(1, Dr), gw, gb, sp]
    if reverse:
        in_specs += [pl.BlockSpec((1, Tc, C), lambda b, n, c: (b, tmap(c), n)),
                     pl.BlockSpec((1, Tc, C), lambda b, n, c: (b, tmap(c), n))]
        args += [hf, u3]
        out_dtype = BF16
    else:
        out_dtype = F32
    return pl.pallas_call(
        functools.partial(_rglru_body, reverse=reverse, Tc=Tc, nb=nb, bs=bs),
        grid=(B, ncb, nc),
        in_specs=in_specs,
        out_specs=pl.BlockSpec((1, Tc, C), lambda b, n, c: (b, tmap(c), n)),
        out_shape=jax.ShapeDtypeStruct((B, S, Dr), out_dtype),
        scratch_shapes=[pltpu.VMEM((SUBLANE, C), F32)],
        compiler_params=_cparams(("parallel", "parallel", "arbitrary")),
        name="rglru_bwd" if reverse else "rglru_fwd",
    )(*args)


def _mixer_rglru(h2, B, S, p, x2, gt):
    u = _matmul(h2, p["w_in"], S=S)
    Dr = u.shape[1] // 2
    u3 = u.reshape(B, S, 2 * Dr)
    hf = _rglru_dir(u3, p["conv_w"], p["conv_b"], p["gw"][0], p["gb"][0], p["sp"][0:1], False)
    y = _rglru_dir(u3, p["conv_w"], p["conv_b"], p["gw"][1], p["gb"][1], p["sp"][1:2], True, hf=hf)
    return _matmul(y.reshape(B * S, Dr), p["w_out"], res=x2, gate=gt, S=S)


def _hgrn2_body(*refs, reverse, Tc, C, HB, K, V):
    if reverse:
        q_ref, z_ref, v_ref, lb_ref, of_ref, g_ref, gn_ref, o_ref, st_ref = refs
    else:
        q_ref, z_ref, v_ref, lb_ref, o_ref, st_ref = refs
    c = pl.program_id(2)

    @pl.when(c == 0)
    def _():
        st_ref[...] = jnp.zeros_like(st_ref)

    dn_t = (((1,), (1,)), ((), ()))
    dn_c = (((0,), (0,)), ((), ()))
    row = lax.broadcasted_iota(jnp.int32, (C, C), 0)
    col = lax.broadcasted_iota(jnp.int32, (C, C), 1)
    keep = (col >= row) if reverse else (col <= row)
    rowt = lax.broadcasted_iota(jnp.int32, (Tc, Tc), 0)
    colt = lax.broadcasted_iota(jnp.int32, (Tc, Tc), 1)
    causal = (colt >= rowt) if reverse else (colt <= rowt)
    tri = jnp.where(causal & ((rowt // C) == (colt // C)), 1.0, 0.0).astype(BF16)
    mid = C // 2
    nsub = Tc // C
    order = list(range(nsub - 1, -1, -1) if reverse else range(nsub))
    heads = range(HB)
    ks = [slice(h * K, (h + 1) * K) for h in heads]
    vs = [slice(h * V, (h + 1) * V) for h in heads]

    lb = lb_ref[...]
    f = lb + (1.0 - lb) * _sigmoid(z_ref[0])
    kk = 1.0 - f
    q = q_ref[0]
    vb = v_ref[0].astype(BF16)
    b = _dot_exact_lhs(tri, jnp.log(f))
    qt, kt, qe, kh, gend = {}, {}, {}, {}, {}
    for j in order:
        sl = slice(j * C, (j + 1) * C)
        bj = b[sl]
        bm = bj[mid:mid + 1]
        b_end = bj[0:1] if reverse else bj[C - 1:C]
        qt[j] = (q[sl] * jnp.exp(jnp.minimum(bj - bm, EXP_CLAMP))).astype(BF16)
        kt[j] = (kk[sl] * jnp.exp(jnp.minimum(bm - bj, EXP_CLAMP))).astype(BF16)
        qe[j] = (q[sl] * jnp.exp(bj)).astype(BF16)
        kh[j] = (kk[sl] * jnp.exp(b_end - bj)).astype(BF16)
        gend[j] = jnp.exp(b_end)
    sc = {(j, h): lax.dot_general(qt[j][:, ks[h]], kt[j][:, ks[h]], dn_t, preferred_element_type=F32)
          for j in order for h in heads}
    dst = {(j, h): lax.dot_general(vb[j * C:(j + 1) * C, vs[h]], kh[j][:, ks[h]], dn_c,
                                   preferred_element_type=F32)
           for j in order for h in heads}
    scm = {jh: jnp.where(keep, sc[jh], 0.0).astype(BF16) for jh in sc}
    ov = {(j, h): jnp.dot(scm[(j, h)], vb[j * C:(j + 1) * C, vs[h]], preferred_element_type=F32)
          for j in order for h in heads}

    st = [st_ref[h] for h in heads]
    for j in order:
        sl = slice(j * C, (j + 1) * C)
        outs = []
        for h in heads:
            oi = lax.dot_general(qe[j][:, ks[h]], st[h].astype(BF16), dn_t, preferred_element_type=F32)
            st[h] = st[h] * gend[j][:, ks[h]] + dst[(j, h)]
            o = oi + ov[(j, h)]
            if reverse:
                ot = of_ref[0, sl, vs[h]] + o
                ms = jnp.mean(ot * ot, axis=-1, keepdims=True)
                o = ot * lax.rsqrt(ms + NORM_EPS) * gn_ref[...]
            outs.append(o)
        oall = outs[0] if HB == 1 else jnp.concatenate(outs, axis=1)
        if reverse:
            o_ref[0, sl, :] = (oall * _silu(g_ref[0, sl, :])).astype(o_ref.dtype)
        else:
            o_ref[0, sl, :] = oall
    for h in heads:
        st_ref[h] = st[h]


def _hgrn2_dir(u3, lb, gn_g, H, K, V, reverse, of=None):
    B, S, _ = u3.shape
    Tc = _pick(S, (256, 128, 64, 32))
    C = min(HG_CHUNK, Tc)
    HB = _pick(H, (2, 1))
    nc = S // Tc
    nh = H // HB
    zoff = (2 if reverse else 1) * nh
    voff = (3 * H * K) // (HB * V)

    def tmap(c):
        return nc - 1 - c if reverse else c

    in_specs = [
        pl.BlockSpec((1, Tc, HB * K), lambda b, h, c: (b, tmap(c), h)),
        pl.BlockSpec((1, Tc, HB * K), lambda b, h, c: (b, tmap(c), zoff + h)),
        pl.BlockSpec((1, Tc, HB * V), lambda b, h, c: (b, tmap(c), voff + h)),
        pl.BlockSpec((1, HB * K), lambda b, h, c: (0, h)),
    ]
    args = [u3, u3, u3, lb]
    if reverse:
        in_specs += [pl.BlockSpec((1, Tc, HB * V), lambda b, h, c: (b, tmap(c), h)),
                     pl.BlockSpec((1, Tc, HB * V), lambda b, h, c: (b, tmap(c), voff + nh + h)),
                     pl.BlockSpec((1, V), lambda b, h, c: (0, 0))]
        args += [of, u3, gn_g]
        out_dtype = BF16
    else:
        out_dtype = F32
    return pl.pallas_call(
        functools.partial(_hgrn2_body, reverse=reverse, Tc=Tc, C=C, HB=HB, K=K, V=V),
        grid=(B, nh, nc),
        in_specs=in_specs,
        out_specs=pl.BlockSpec((1, Tc, HB * V), lambda b, h, c: (b, tmap(c), h)),
        out_shape=jax.ShapeDtypeStruct((B, S, H * V), out_dtype),
        scratch_shapes=[pltpu.VMEM((HB, V, K), F32)],
        compiler_params=_cparams(("parallel", "parallel", "arbitrary")),
        name="hgrn2_bwd" if reverse else "hgrn2_fwd",
    )(*args)


def _mixer_hgrn2(h2, B, S, p, lb, x2, gt):
    D = h2.shape[1]
    V = p["gn_g"].shape[-1]
    H = D // V
    K = lb.shape[-1] // H
    u = _matmul(h2, p["w_in"], S=S)
    u3 = u.reshape(B, S, u.shape[1])
    of = _hgrn2_dir(u3, lb, p["gn_g"], H, K, V, False)
    y = _hgrn2_dir(u3, lb, p["gn_g"], H, K, V, True, of=of)
    return _matmul(y.reshape(B * S, D), p["w_out"], res=x2, gate=gt, S=S)


def _rwkv_scan_body(*refs, Tc, N, VH, reverse):
    if reverse:
        (r_ref, k_ref, lw_ref, la_ref, v_ref, w0_ref, a0_ref, kkp_ref, kap_ref, rkp_ref,
         yf_ref, bof_ref, lg_ref, lb_ref, o_ref, st_ref, op_ref, bon_ref) = refs
    else:
        (r_ref, k_ref, lw_ref, la_ref, v_ref, w0_ref, a0_ref, kkp_ref, kap_ref, rkp_ref,
         y_ref, bo_ref, st_ref, op_ref, bon_ref) = refs
    yo_ref = o_ref if reverse else y_ref
    c = pl.program_id(1)

    @pl.when(c == 0)
    def _():
        st_ref[...] = jnp.zeros_like(st_ref)

    def vsum(x):
        s = jnp.sum(x, axis=1)
        for b in range(1, VH):
            s = s + pltpu.roll(s, b * (LANE // VH), axis=1)
        return s[:, None, :]

    x = w0_ref[...] + lw_ref[...]
    wl = -(jnp.maximum(-x, 0.0) + jnp.log1p(jnp.exp(-jnp.abs(x)))) - 0.5
    op_ref[0] = jnp.exp(-jnp.exp(wl))
    a = _sigmoid(a0_ref[...] + la_ref[...])
    kt = k_ref[...]
    kd = kt * (1.0 + (a - 1.0) * kap_ref[...])
    op_ref[3] = kd
    kx = kt * kkp_ref[...]
    nrm = jnp.sqrt(jnp.sum(kx * kx, axis=1, keepdims=True))
    kap = kx / jnp.maximum(nrm, 1e-12)
    op_ref[1] = -kap
    op_ref[2] = kap * a
    bon_ref[...] = jnp.sum(r_ref[...] * kd * rkp_ref[...], axis=1, keepdims=True)

    def first_t():
        return Tc - 1 if reverse else 0

    t0 = first_t()
    sa0 = st_ref[0] * op_ref[1, t0, 0:1, :]
    for k in range(1, N):
        sa0 = sa0 + st_ref[k] * op_ref[1, t0, k:k + 1, :]

    def step(i, sa):
        t = Tc - 1 - i if reverse else i
        tn = jnp.clip(t - 1 if reverse else t + 1, 0, Tc - 1)
        vv = v_ref[t]
        ys = [None, None]
        sas = [None, None]
        for k in range(N):
            s_new = (st_ref[k] * op_ref[0, t, k:k + 1, :] + sa * op_ref[2, t, k:k + 1, :]
                     + vv * op_ref[3, t, k:k + 1, :])
            st_ref[k] = s_new
            yk = s_new * r_ref[t, k:k + 1, :]
            sk = s_new * op_ref[1, tn, k:k + 1, :]
            ys[k % 2] = yk if ys[k % 2] is None else ys[k % 2] + yk
            sas[k % 2] = sk if sas[k % 2] is None else sas[k % 2] + sk
        yo_ref[t] = ys[0] + ys[1]
        return sas[0] + sas[1]

    lax.fori_loop(0, Tc, step, sa0)

    bonus = bon_ref[...] * v_ref[...]
    if reverse:
        yt = yf_ref[...] + o_ref[...]
        yc = yt - vsum(yt) * (1.0 / N)
        var = vsum(yc * yc) * (1.0 / N)
        o_ref[...] = yc * lax.rsqrt(var + LNX_EPS) * lg_ref[...] + lb_ref[...] + bof_ref[...] + bonus
    else:
        bo_ref[...] = bonus


def _rwkv_scan(kops, v, kpars, VH, reverse, yf=None, bof=None, vpars=None):
    S, N, NCp = kops[0].shape
    NV = N // VH
    Tc = _pick(S, (32, 16, 8))
    nc = S // Tc
    tmap = (lambda g, c: (nc - 1 - c, 0, g)) if reverse else (lambda g, c: (c, 0, g))
    kspec = pl.BlockSpec((Tc, N, LANE), tmap)
    vspec = pl.BlockSpec((Tc, NV, LANE), tmap)
    kpar = pl.BlockSpec((N, LANE), lambda g, c: (0, g))
    vpar = pl.BlockSpec((NV, LANE), lambda g, c: (0, g))
    in_specs = [kspec] * 4 + [vspec] + [kpar] * 5
    args = list(kops) + [v] + list(kpars)
    vshape = jax.ShapeDtypeStruct((S, NV, NCp), F32)
    if reverse:
        in_specs += [vspec, vspec, vpar, vpar]
        args += [yf, bof] + list(vpars)
        out_specs, out_shape = vspec, vshape
    else:
        out_specs, out_shape = (vspec, vspec), (vshape, vshape)
    return pl.pallas_call(
        functools.partial(_rwkv_scan_body, Tc=Tc, N=N, VH=VH, reverse=reverse),
        grid=(NCp // LANE, nc),
        in_specs=in_specs,
        out_specs=out_specs,
        out_shape=out_shape,
        scratch_shapes=[pltpu.VMEM((N, NV, LANE), F32), pltpu.VMEM((4, Tc, N, LANE), F32),
                        pltpu.VMEM((Tc, 1, LANE), F32)],
        compiler_params=_cparams(("parallel", "arbitrary")),
        name="rwkv7_scan_bwd" if reverse else "rwkv7_scan_fwd",
    )(*args)


def _mixer_rwkv7(h2, B, S, p, x2, gt):
    T, D = h2.shape
    H, N = p["r_k"].shape
    h3 = h2.reshape(B, S, D)
    zero = jnp.zeros_like(h3[:, :1])
    prev = jnp.concatenate([zero, h3[:, :-1]], axis=1)
    nxt = jnp.concatenate([h3[:, 1:], zero], axis=1)
    xx = (0.5 * (prev + nxt) - h3).reshape(T, D)
    mu = p["mu"]
    mm = lambda n, w, **kw: _matmul(h2, w, mix=(xx, mu[n]), S=S, **kw)
    r = mm(0, p["w_r"])
    k = mm(2, p["w_k"])
    v = mm(3, p["w_v"])
    Rp = p["w2"].shape[1]
    Rg = p["g2"].shape[0]
    tl = _matmul(h2, p["lora_in"], mix=(xx, p["lora_mu"]), S=S, tn=Rg, lhs_major=True)
    lw = [_matmul(tl, p["w2"][d], act="tanh", S=S, x_col_block=d) for d in range(2)]
    la = [_matmul(tl, p["a2"][d], S=S, x_col_block=Rg // Rp + d) for d in range(2)]
    g = _matmul(tl, p["g2"], act="sigmoid", S=S, x_col_block=2)

    NC = B * H
    VH = 2 if NC * 2 == LANE else 1
    NV = N // VH
    pad = (-VH * NC) % LANE

    def lanes(t):
        return jnp.pad(t, [(0, 0)] * (t.ndim - 1) + [(0, pad)]) if pad else t

    def kidx(t):
        return lanes(jnp.concatenate([t] * VH, axis=-1) if VH > 1 else t)

    def vidx(t):
        lead = t.shape[:-2]
        t = t.reshape(*lead, VH, NV, NC)
        t = jnp.moveaxis(t, -3, -2).reshape(*lead, NV, VH * NC)
        return lanes(t)

    def chains(t2):
        return t2.reshape(B, S, H, N).transpose(1, 3, 0, 2).reshape(S, N, NC)

    def par(t):
        return jnp.tile(t.reshape(H, N).T, (1, B))

    rc, kc, vc = kidx(chains(r)), kidx(chains(k)), vidx(chains(v))
    shared = (kidx(par(p["k_k"])), kidx(par(p["k_a"])), kidx(par(p["r_k"])))
    yf, bof = _rwkv_scan((rc, kc, kidx(chains(lw[0])), kidx(chains(la[0]))), vc,
                         (kidx(par(p["w0"][0])), kidx(par(p["a0"][0]))) + shared, VH, False)
    oc = _rwkv_scan((rc, kc, kidx(chains(lw[1])), kidx(chains(la[1]))), vc,
                    (kidx(par(p["w0"][1])), kidx(par(p["a0"][1]))) + shared, VH, True,
                    yf=yf, bof=bof, vpars=(vidx(par(p["lnx_g"])), vidx(par(p["lnx_b"]))))
    oc = oc[:, :, :VH * NC].reshape(S, NV, VH, NC).transpose(0, 2, 1, 3).reshape(S, N, B, H)
    out = oc.transpose(2, 0, 3, 1).reshape(T, D)
    return _matmul(out, p["w_o"], mul=g, res=x2, gate=gt, S=S)


def _ffn_body(x_ref, wg_ref, wu_ref, wd_ref, gt_ref, o_ref, acc_ref):
    f = pl.program_id(2)
    x = x_ref[0]
    g = jnp.dot(x, wg_ref[0], preferred_element_type=F32)
    u = jnp.dot(x, wu_ref[0], preferred_element_type=F32)
    hid = (_silu(g) * u).astype(BF16)
    part = jnp.dot(hid, wd_ref[0], preferred_element_type=F32)

    @pl.when(f == 0)
    def _():
        acc_ref[...] = part

    @pl.when(f > 0)
    def _():
        acc_ref[...] = acc_ref[...] + part

    @pl.when(f == pl.num_programs(2) - 1)
    def _():
        o_ref[0] = (acc_ref[...] * gt_ref[0]).astype(o_ref.dtype)


FFN_VMEM_LIMIT = 56 * 1024 * 1024


def _moe_ffn(xe, wg, wu, wd, gates):
    E, cap, D = xe.shape
    Fd = wg.shape[2]
    tm = _pick(cap, (1024, 512, 256, 128, 64, 32, 16, 8))
    tf = _pick(Fd, (512, 256, 128))
    return pl.pallas_call(
        _ffn_body,
        grid=(E, cap // tm, Fd // tf),
        in_specs=[
            pl.BlockSpec((1, tm, D), lambda e, m, f: (e, m, 0)),
            pl.BlockSpec((1, D, tf), lambda e, m, f: (e, 0, f)),
            pl.BlockSpec((1, D, tf), lambda e, m, f: (e, 0, f)),
            pl.BlockSpec((1, tf, D), lambda e, m, f: (e, f, 0)),
            pl.BlockSpec((1, tm, 1), lambda e, m, f: (e, m, 0)),
        ],
        out_specs=pl.BlockSpec((1, tm, D), lambda e, m, f: (e, m, 0)),
        out_shape=jax.ShapeDtypeStruct((E, cap, D), BF16),
        scratch_shapes=[pltpu.VMEM((tm, D), F32)],
        compiler_params=pltpu.CompilerParams(dimension_semantics=("parallel", "parallel", "arbitrary"),
                                             vmem_limit_bytes=FFN_VMEM_LIMIT),
        name="moe_ffn",
    )(xe, wg, wu, wd, gates.reshape(E, cap, 1))


COMB_TT = 256
COMB_W = 64


def _combine_body(off_ref, x_ref, gt_ref, ye_hbm, tok_hbm, o_ref, ybuf, tbuf, xbuf, xtok, acc_ref, sem, xsem,
                  *, E, W, TT, cap):
    i = pl.program_id(0)
    n = pl.num_programs(0)
    slot = i % 2

    ROWS = 2 * SUBLANE

    def win_start(e, tile):
        st = jnp.minimum(off_ref[e, tile], cap - W)
        return pl.multiple_of((st // ROWS) * ROWS, ROWS)

    def ye_copy(e, tile, sl):
        return pltpu.make_async_copy(ye_hbm.at[e, pl.ds(win_start(e, tile), W), :],
                                     ybuf.at[sl, pl.ds(e * W, W), :], sem.at[sl, 0])

    def tok_copy(e, tile, sl):
        return pltpu.make_async_copy(tok_hbm.at[e, pl.ds(win_start(e, tile), W), :],
                                     tbuf.at[sl, pl.ds(e * W, W), :], sem.at[sl, 1])

    def fetch(tile, sl):
        for e in range(E):
            ye_copy(e, tile, sl).start()
            tok_copy(e, tile, sl).start()

    @pl.when(i == 0)
    def _():
        fetch(0, 0)

    @pl.when(i + 1 < n)
    def _():
        fetch(i + 1, 1 - slot)

    for e in range(E):
        ye_copy(e, i, slot).wait()
        tok_copy(e, i, slot).wait()

    dn_c = (((0,), (0,)), ((), ()))
    base = i * TT

    def onehot(tok, rows):
        lane = lax.broadcasted_iota(jnp.int32, (rows, LANE), 1)
        parts = [jnp.where(tok == base + j * LANE + lane, 1.0, 0.0) for j in range(TT // LANE)]
        return jnp.concatenate(parts, axis=1).astype(BF16)

    def scatter(oh, y):
        return lax.dot_general(oh, y, dn_c, preferred_element_type=F32)

    acc_ref[...] = scatter(onehot(tbuf[slot], E * W), ybuf[slot])

    def extra(e, carry):
        hi_s = off_ref[e, i + 1]

        def cond(s):
            return s < hi_s

        def body(s):
            st = pl.multiple_of(jnp.minimum(s, cap - W), ROWS)
            cy = pltpu.make_async_copy(ye_hbm.at[e, pl.ds(st, W), :], xbuf, xsem.at[0])
            ct = pltpu.make_async_copy(tok_hbm.at[e, pl.ds(st, W), :], xtok, xsem.at[1])
            cy.start()
            ct.start()
            cy.wait()
            ct.wait()
            rowi = lax.broadcasted_iota(jnp.int32, (W, LANE), 0)
            tok = jnp.where(st + rowi >= s, xtok[...], -1)
            acc_ref[...] = acc_ref[...] + scatter(onehot(tok, W), xbuf[...])
            return s + W

        lax.while_loop(cond, body, win_start(e, i) + W)
        return carry

    lax.fori_loop(0, E, extra, 0)
    o_ref[...] = x_ref[...] + gt_ref[0] * acc_ref[...]


def _combine(x2, gt, ye, tok, off, S):
    T, D = x2.shape
    E, cap, _ = ye.shape
    B = gt.shape[0]
    TT = _pick(S, (COMB_TT, LANE))
    W = min(COMB_W, cap)
    grid_spec = pltpu.PrefetchScalarGridSpec(
        num_scalar_prefetch=1,
        grid=(T // TT,),
        in_specs=[
            pl.BlockSpec((TT, D), lambda i, off: (i, 0)),
            pl.BlockSpec((1, 1, D), lambda i, off: ((i * TT) // S, 0, 0)),
            pl.BlockSpec(memory_space=pl.ANY),
            pl.BlockSpec(memory_space=pl.ANY),
        ],
        out_specs=pl.BlockSpec((TT, D), lambda i, off: (i, 0)),
        scratch_shapes=[
            pltpu.VMEM((2, E * W, D), BF16),
            pltpu.VMEM((2, E * W, LANE), jnp.int32),
            pltpu.VMEM((W, D), BF16),
            pltpu.VMEM((W, LANE), jnp.int32),
            pltpu.VMEM((TT, D), F32),
            pltpu.SemaphoreType.DMA((2, 2)),
            pltpu.SemaphoreType.DMA((2,)),
        ],
    )
    return pl.pallas_call(
        functools.partial(_combine_body, E=E, W=W, TT=TT, cap=cap),
        grid_spec=grid_spec,
        out_shape=jax.ShapeDtypeStruct((T, D), F32),
        compiler_params=_cparams(("arbitrary",)),
        name="moe_combine",
    )(off, x2, gt.reshape(B, 1, D), ye, tok)


def _ec_moe(x2, gt, h2, probs_t, p, S):
    T, D = h2.shape
    E = probs_t.shape[0]
    cap = CAPACITY_FACTOR * T // E
    gates, idx = lax.top_k(probs_t, cap)
    idx, gates = lax.sort_key_val(idx, gates, dimension=1)
    xe = h2.at[idx].get(mode="promise_in_bounds")
    ye = _moe_ffn(xe, p["w_gate"], p["w_up"], p["w_down"], gates)
    TT = _pick(S, (COMB_TT, LANE))
    starts = jnp.arange(T // TT + 1, dtype=jnp.int32) * TT
    off = jnp.sum(idx[:, None, :] < starts[None, :, None], axis=-1, dtype=jnp.int32)
    tok = jnp.broadcast_to(idx[:, :, None], (E, cap, LANE))
    return _combine(x2, gt, ye, tok, off, S)


def _trunk(x, mod, P):
    B, S, D = x.shape
    T = B * S
    x2 = x.reshape(T, D)
    L = mod.shape[0]
    for layer in range(L):
        sh1, sc1, gt1, sh2, sc2, gt2 = jnp.split(mod[layer], 6, axis=-1)
        kind, j = layer % 3, layer // 3
        mix_dtype = F32 if kind == 2 else BF16
        h2 = _norm_mod(x2, P["norm_g"][layer, 0], sc1, sh1, S, mix_dtype)
        if kind == 0:
            x2 = _mixer_rglru(h2, B, S, P["lru"][j], x2, gt1)
        elif kind == 1:
            x2 = _mixer_hgrn2(h2, B, S, P["hg"][j], P["lbs"][layer:layer + 1], x2, gt1)
        else:
            x2 = _mixer_rwkv7(h2, B, S, P["r7"][j], x2, gt1)
        h2, probs_t = _norm_mod(x2, P["norm_g"][layer, 1], sc2, sh2, S, BF16,
                                router_wt=P["moe"][layer]["router_t"])
        x2 = _ec_moe(x2, gt2, h2, probs_t, P["moe"][layer], S)
    zeros = jnp.zeros((B, D), F32)
    y = _norm_mod(x2, P["final_g"], zeros, zeros, S, F32)
    return y.reshape(B, S, D)


def _pad_cols(w, width):
    return jnp.pad(w, ((0, 0), (0, width - w.shape[1])))


def _pad_rows(w, height):
    return jnp.pad(w, ((0, height - w.shape[0]), (0, 0)))


def kernel(x_prompt, x_sample, c_prompt, c_sample, ada_w, ada_b, norm_g, final_g, lru_w_in, lru_conv_w, lru_conv_b, lru_gate_w, lru_gate_b, lru_lam, lru_w_out, hg_w_in, hg_lb, hg_gn_g, hg_w_out, r7_mu, r7_w_rkv, r7_w0, r7_w1, r7_w2, r7_a0, r7_a1, r7_a2, r7_g1, r7_g2, r7_k_k, r7_k_a, r7_r_k, r7_lnx_g, r7_lnx_b, r7_w_o, moe_router, moe_w_gate, moe_w_up, moe_w_down):
    L = ada_w.shape[0]
    bf = lambda t: t.astype(BF16)

    lower = jnp.cumsum(jax.nn.softmax(hg_lb.astype(F32), axis=0), axis=0)
    P = {"norm_g": norm_g, "final_g": final_g, "lbs": lower - lower[:1]}
    P["lru"] = [dict(w_in=bf(lru_w_in[j]), conv_w=lru_conv_w[j], conv_b=lru_conv_b[j],
                     gw=bf(lru_gate_w[j]), gb=lru_gate_b[j], sp=jax.nn.softplus(-lru_lam[j]),
                     w_out=bf(lru_w_out[j])) for j in range(lru_w_in.shape[0])]
    P["hg"] = [dict(w_in=bf(hg_w_in[j]), gn_g=hg_gn_g[j].reshape(1, -1), w_out=bf(hg_w_out[j]))
               for j in range(hg_w_in.shape[0])]
    P["r7"] = []
    for j in range(r7_mu.shape[0]):
        R = r7_w1.shape[-1]
        Rp = -(-R // LANE) * LANE
        assert 2 * Rp == r7_g1.shape[-1], "the merged LoRA-input projection needs equally wide parts"
        P["r7"].append(dict(
            mu=r7_mu[j], w_r=bf(r7_w_rkv[j, 0]), w_k=bf(r7_w_rkv[j, 1]), w_v=bf(r7_w_rkv[j, 2]),
            w0=r7_w0[j], a0=r7_a0[j],
            lora_in=bf(jnp.concatenate([_pad_cols(r7_w1[j, d], Rp) for d in range(2)]
                                       + [_pad_cols(r7_a1[j, d], Rp) for d in range(2)] + [r7_g1[j]], axis=1)),
            lora_mu=jnp.stack([r7_mu[j, 1], r7_mu[j, 4], r7_mu[j, 5]]),
            w2=bf(jnp.stack([_pad_rows(r7_w2[j, d], Rp) for d in range(2)])),
            a2=bf(jnp.stack([_pad_rows(r7_a2[j, d], Rp) for d in range(2)])),
            g2=bf(r7_g2[j]), k_k=r7_k_k[j], k_a=r7_k_a[j], r_k=r7_r_k[j],
            lnx_g=r7_lnx_g[j], lnx_b=r7_lnx_b[j], w_o=bf(r7_w_o[j])))
    P["moe"] = [dict(router_t=moe_router[l].T, w_gate=bf(moe_w_gate[l]), w_up=bf(moe_w_up[l]),
                     w_down=bf(moe_w_down[l])) for l in range(L)]

    Bp, Bs = c_prompt.shape[0], c_sample.shape[0]
    c_all = jnp.concatenate([c_prompt, c_sample], axis=0)
    pad = (-c_all.shape[0]) % SUBLANE
    c_all = jnp.pad(c_all, ((0, pad), (0, 0)))
    mod = _ada_mod(c_all, ada_w, ada_b)
    y_prompt = _trunk(x_prompt, mod[:, :Bp], P)
    y_sample = _trunk(x_sample, mod[:, Bp:Bp + Bs], P)
    return (y_prompt, y_sample)
```
